```python
import math
import jax, jax.numpy as jnp
from jax import lax
import numpy as np

D_MODEL = 1024
BATCH = 1
SEQ = 16384
DEPTH = 4
DEC_BATCH = 2
DEC_SEQ = 16384
PAST_LEN = 128

N_MIXERS = 2
N_ATTN_LAYERS = (DEPTH + 1) // 2
N_GDN_LAYERS = DEPTH // 2
RMS_EPS = 1e-6

ATTN_HEAD_DIM = 128
ATTN_Q_HEADS = D_MODEL // ATTN_HEAD_DIM
ATTN_KV_HEADS = 2
ATTN_GROUP = ATTN_Q_HEADS // ATTN_KV_HEADS
ATTN_Q_DIM = ATTN_Q_HEADS * ATTN_HEAD_DIM
ATTN_KV_DIM = ATTN_KV_HEADS * ATTN_HEAD_DIM
ATTN_IN = ATTN_Q_DIM + 2 * ATTN_KV_DIM
WINDOW = 128
ATTN_BLOCK = 128
ROPE_THETA = 10000.0

GDN_HEAD_DIM = 128
GDN_K_HEADS = D_MODEL // GDN_HEAD_DIM
GDN_V_HEADS = 2 * GDN_K_HEADS
GDN_QK_DIM = GDN_K_HEADS * GDN_HEAD_DIM
GDN_V_DIM = GDN_V_HEADS * GDN_HEAD_DIM
GDN_CONV_DIM = 2 * GDN_QK_DIM + GDN_V_DIM
GDN_GATE_DIM = 4 * GDN_V_HEADS
GDN_IN = GDN_CONV_DIM + GDN_V_DIM + GDN_GATE_DIM
CONV_WIDTH = 5
CONV_PAD = CONV_WIDTH // 2
CHUNK = 64

D_FF = 4 * D_MODEL

kernel_name = "hybrid_bidir_swa_gdn_encoder"


def rms_norm(x, w):
    xf = x.astype(jnp.float32)
    y = xf * lax.rsqrt(jnp.mean(xf * xf, axis=-1, keepdims=True) + RMS_EPS)
    return (y * w.astype(jnp.float32)).astype(x.dtype)


def l2_normalize(x):
    xf = x.astype(jnp.float32)
    return xf * lax.rsqrt(jnp.sum(xf * xf, axis=-1, keepdims=True) + RMS_EPS)


def rope(x, pos):
    half = x.shape[-1] // 2
    inv_freq = 1.0 / (ROPE_THETA ** (jnp.arange(half, dtype=jnp.float32) / half))
    ang = pos[:, None] * inv_freq[None, :]
    cos = jnp.cos(ang)[:, None, :]
    sin = jnp.sin(ang)[:, None, :]
    xf = x.astype(jnp.float32)
    x1, x2 = xf[..., :half], xf[..., half:]
    return jnp.concatenate([x1 * cos - x2 * sin, x2 * cos + x1 * sin], axis=-1).astype(x.dtype)


def window_gqa(h, w_in, sinks, w_out):
    B, S, _ = h.shape
    nb = S // ATTN_BLOCK
    proj = h @ w_in
    q = proj[..., :ATTN_Q_DIM].reshape(B, S, ATTN_Q_HEADS, ATTN_HEAD_DIM)
    k = proj[..., ATTN_Q_DIM:ATTN_Q_DIM + ATTN_KV_DIM].reshape(B, S, ATTN_KV_HEADS, ATTN_HEAD_DIM)
    v = proj[..., ATTN_Q_DIM + ATTN_KV_DIM:].reshape(B, S, ATTN_KV_HEADS, ATTN_HEAD_DIM)
    pos = jnp.arange(S, dtype=jnp.float32)
    q = rope(q, pos)
    k = rope(k, pos)
    qb = q.reshape(B, nb, ATTN_BLOCK, ATTN_KV_HEADS, ATTN_GROUP, ATTN_HEAD_DIM)
    pad = ((0, 0), (ATTN_BLOCK, ATTN_BLOCK), (0, 0), (0, 0))
    kb = jnp.pad(k, pad).reshape(B, nb + 2, ATTN_BLOCK, ATTN_KV_HEADS, ATTN_HEAD_DIM)
    vb = jnp.pad(v, pad).reshape(B, nb + 2, ATTN_BLOCK, ATTN_KV_HEADS, ATTN_HEAD_DIM)
    kn = jnp.concatenate([kb[:, :-2], kb[:, 1:-1], kb[:, 2:]], axis=2)
    vn = jnp.concatenate([vb[:, :-2], vb[:, 1:-1], vb[:, 2:]], axis=2)
    scale = ATTN_HEAD_DIM ** -0.5
    s = jnp.einsum('bnqhgd,bnkhd->bnhgqk', qb, kn, preferred_element_type=jnp.float32) * scale
    qpos = jnp.arange(S).reshape(nb, ATTN_BLOCK)
    kpos = (jnp.arange(nb)[:, None] - 1) * ATTN_BLOCK + jnp.arange(3 * ATTN_BLOCK)[None, :]
    valid = ((jnp.abs(qpos[:, :, None] - kpos[:, None, :]) <= WINDOW)
             & (kpos[:, None, :] >= 0) & (kpos[:, None, :] < S))
    s = jnp.where(valid[None, :, None, None], s, -jnp.inf)
    sink = sinks.astype(jnp.float32).reshape(ATTN_KV_HEADS, ATTN_GROUP)[None, None, :, :, None, None]
    m = jnp.maximum(jnp.max(s, axis=-1, keepdims=True), sink)
    p = jnp.exp(s - m)
    denom = jnp.sum(p, axis=-1, keepdims=True) + jnp.exp(sink - m)
    p = (p / denom).astype(h.dtype)
    o = jnp.einsum('bnhgqk,bnkhd->bnqhgd', p, vn)
    return o.reshape(B, S, ATTN_Q_DIM) @ w_out


def depthwise_conv_centred(x, w):
    C = x.shape[-1]
    return lax.conv_general_dilated(
        x, w[:, None, :].astype(x.dtype), window_strides=(1,), padding=[(CONV_PAD, CONV_PAD)],
        dimension_numbers=('NWC', 'WIO', 'NWC'), feature_group_count=C)


def gated_delta_chunked(q, k, v, g, beta):
    B, H, S, DK = q.shape
    DV = v.shape[-1]
    n = S // CHUNK
    q = q.reshape(B, H, n, CHUNK, DK)
    k = k.reshape(B, H, n, CHUNK, DK)
    v = v.reshape(B, H, n, CHUNK, DV)
    g = jnp.cumsum(g.reshape(B, H, n, CHUNK), axis=-1)
    beta = beta.reshape(B, H, n, CHUNK)
    kb = k * beta[..., None]
    vb = v * beta[..., None]
    idx = jnp.arange(CHUNK)
    lower_incl = idx[:, None] >= idx[None, :]
    strict = idx[:, None] > idx[None, :]
    decay = jnp.exp(jnp.where(lower_incl, g[..., :, None] - g[..., None, :], -jnp.inf))
    L = jnp.where(strict, jnp.einsum('bhncd,bhnsd->bhncs', kb, k) * decay, 0.0)
    eye = jnp.broadcast_to(jnp.eye(CHUNK, dtype=L.dtype), L.shape)
    T = lax.linalg.triangular_solve(L, eye, left_side=True, lower=True, unit_diagonal=True)
    u = jnp.einsum('bhncs,bhnse->bhnce', T, vb)
    w = jnp.einsum('bhncs,bhnsd->bhncd', T, kb * jnp.exp(g)[..., None])
    a_qk = jnp.einsum('bhncd,bhnsd->bhncs', q, k) * decay

    def step(state, xs):
        qi, ki, ui, wi, gi, ai = xs
        v_new = ui - jnp.einsum('bhcd,bhde->bhce', wi, state)
        o = (jnp.einsum('bhcd,bhde->bhce', qi * jnp.exp(gi)[..., None], state)
             + jnp.einsum('bhcs,bhse->bhce', ai, v_new))
        g_last = gi[..., -1]
        state = (state * jnp.exp(g_last)[..., None, None]
                 + jnp.einsum('bhcd,bhce->bhde', ki * jnp.exp(g_last[..., None] - gi)[..., None], v_new))
        return state, o

    xs = tuple(jnp.moveaxis(t, 2, 0) for t in (q, k, u, w, g, a_qk))
    state0 = jnp.zeros((B, H, DK, DV), jnp.float32)
    _, o = lax.scan(step, state0, xs)
    return jnp.moveaxis(o, 0, 2).reshape(B, H, S, DV)


def gated_deltanet_bidir(h, w_in, conv_w, A_log, dt_bias, norm_w, w_out):
    B, S, _ = h.shape
    proj = h @ w_in
    qkv = jax.nn.silu(depthwise_conv_centred(proj[..., :GDN_CONV_DIM], conv_w))
    z = proj[..., GDN_CONV_DIM:GDN_CONV_DIM + GDN_V_DIM].reshape(B, S, GDN_V_HEADS, GDN_HEAD_DIM)
    gate_in = proj[..., GDN_CONV_DIM + GDN_V_DIM:].astype(jnp.float32).reshape(B, S, 2, 2, GDN_V_HEADS)
    q = l2_normalize(qkv[..., :GDN_QK_DIM].reshape(B, S, GDN_K_HEADS, GDN_HEAD_DIM)) * (GDN_HEAD_DIM ** -0.5)
    k = l2_normalize(qkv[..., GDN_QK_DIM:2 * GDN_QK_DIM].reshape(B, S, GDN_K_HEADS, GDN_HEAD_DIM))
    v = qkv[..., 2 * GDN_QK_DIM:].reshape(B, S, GDN_V_HEADS, GDN_HEAD_DIM).astype(jnp.float32)
    rep = GDN_V_HEADS // GDN_K_HEADS
    q = jnp.repeat(q, rep, axis=2).transpose(0, 2, 1, 3)
    k = jnp.repeat(k, rep, axis=2).transpose(0, 2, 1, 3)
    v = v.transpose(0, 2, 1, 3)
    beta = jax.nn.sigmoid(gate_in[:, :, :, 0]).transpose(2, 0, 3, 1)
    g = (-jnp.exp(A_log.astype(jnp.float32))[None, None]
         * jax.nn.softplus(gate_in[:, :, :, 1] + dt_bias.astype(jnp.float32)[None, None]))
    g = g.transpose(2, 0, 3, 1)
    o_fwd = gated_delta_chunked(q, k, v, g[0], beta[0])
    flip = lambda t: jnp.flip(t, axis=2)
    o_bwd = flip(gated_delta_chunked(flip(q), flip(k), flip(v), flip(g[1]), flip(beta[1])))
    o = (o_fwd + o_bwd).transpose(0, 2, 1, 3)
    o = rms_norm(o, norm_w) * jax.nn.silu(z.astype(jnp.float32))
    return o.reshape(B, S, GDN_V_DIM).astype(h.dtype) @ w_out


def squared_relu_mlp(h, w_up, w_down):
    return jnp.square(jax.nn.relu(h @ w_up)) @ w_down


def encoder_trunk(x, norm_gains, attn_w_in, attn_sinks, attn_w_out, gdn_w_in, gdn_conv_w,
                  gdn_A_log, gdn_dt_bias, gdn_norm_w, gdn_w_out, mlp_w_up, mlp_w_down):
    for i in range(DEPTH):
        j = i // N_MIXERS
        h = rms_norm(x, norm_gains[i, 0])
        if i % N_MIXERS == 0:
            h = window_gqa(h, attn_w_in[j], attn_sinks[j], attn_w_out[j])
        else:
            h = gated_deltanet_bidir(h, gdn_w_in[j], gdn_conv_w[j], gdn_A_log[j], gdn_dt_bias[j],
                                     gdn_norm_w[j], gdn_w_out[j])
        x = x + rms_norm(h, norm_gains[i, 1])
        h = squared_relu_mlp(rms_norm(x, norm_gains[i, 2]), mlp_w_up[i], mlp_w_down[i])
        x = x + rms_norm(h, norm_gains[i, 3])
    return x


def setup_inputs(seed: int = 0) -> dict:
    key = jax.random.key(seed)
    ks = jax.random.split(key, 16)
    f32 = jnp.float32

    def normal(k, shape, scale):
        return jax.random.normal(k, shape, f32) * scale

    x_prompt = normal(ks[0], (BATCH, SEQ, D_MODEL), 1.0)
    x_sample = normal(ks[1], (DEC_BATCH, DEC_SEQ, D_MODEL), 1.0)
    norm_gains = 1.0 + normal(ks[2], (DEPTH, 4, D_MODEL), 0.02)
    attn_w_in = normal(ks[3], (N_ATTN_LAYERS, D_MODEL, ATTN_IN), D_MODEL ** -0.5)
    attn_sinks = normal(ks[4], (N_ATTN_LAYERS, ATTN_Q_HEADS), 0.5)
    attn_w_out = normal(ks[5], (N_ATTN_LAYERS, ATTN_Q_DIM, D_MODEL), ATTN_Q_DIM ** -0.5)
    gdn_w_in = normal(ks[6], (N_GDN_LAYERS, D_MODEL, GDN_IN), D_MODEL ** -0.5)
    gdn_conv_w = normal(ks[7], (N_GDN_LAYERS, CONV_WIDTH, GDN_CONV_DIM), CONV_WIDTH ** -0.5)
    gdn_A_log = jnp.log(jax.random.uniform(ks[8], (N_GDN_LAYERS, 2, GDN_V_HEADS), f32, 1.0, 16.0))
    dt = jnp.exp(jax.random.uniform(ks[9], (N_GDN_LAYERS, 2, GDN_V_HEADS), f32,
                                    math.log(1e-3), math.log(1e-1)))
    gdn_dt_bias = dt + jnp.log(-jnp.expm1(-dt))
    gdn_norm_w = 1.0 + normal(ks[10], (N_GDN_LAYERS, GDN_HEAD_DIM), 0.02)
    gdn_w_out = normal(ks[11], (N_GDN_LAYERS, GDN_V_DIM, D_MODEL), GDN_V_DIM ** -0.5)
    mlp_w_up = normal(ks[12], (DEPTH, D_MODEL, D_FF), D_MODEL ** -0.5)
    mlp_w_down = normal(ks[13], (DEPTH, D_FF, D_MODEL), D_FF ** -0.5)
    return {"x_prompt": x_prompt, "x_sample": x_sample, "norm_gains": norm_gains,
            "attn_w_in": attn_w_in, "attn_sinks": attn_sinks, "attn_w_out": attn_w_out,
            "gdn_w_in": gdn_w_in, "gdn_conv_w": gdn_conv_w, "gdn_A_log": gdn_A_log,
            "gdn_dt_bias": gdn_dt_bias, "gdn_norm_w": gdn_norm_w, "gdn_w_out": gdn_w_out,
            "mlp_w_up": mlp_w_up, "mlp_w_down": mlp_w_down}


def reference(x_prompt, x_sample, norm_gains, attn_w_in, attn_sinks, attn_w_out, gdn_w_in, gdn_conv_w,
              gdn_A_log, gdn_dt_bias, gdn_norm_w, gdn_w_out, mlp_w_up, mlp_w_down):
    y_prompt = encoder_trunk(x_prompt, norm_gains, attn_w_in, attn_sinks, attn_w_out, gdn_w_in, gdn_conv_w,
                             gdn_A_log, gdn_dt_bias, gdn_norm_w, gdn_w_out, mlp_w_up, mlp_w_down)
    y_sample = encoder_trunk(x_sample, norm_gains, attn_w_in, attn_sinks, attn_w_out, gdn_w_in, gdn_conv_w,
                             gdn_A_log, gdn_dt_bias, gdn_norm_w, gdn_w_out, mlp_w_up, mlp_w_down)
    return (y_prompt, y_sample)
```

```python
import functools
import math

import jax
import jax.numpy as jnp
from jax import lax
from jax.experimental import pallas as pl
from jax.experimental.pallas import tpu as pltpu

F32 = jnp.float32
BF16 = jnp.bfloat16

D_MODEL = 1024
D_FF = 4 * D_MODEL
RMS_EPS = 1e-6
HEAD_DIM = 128

ATTN_Q_HEADS = 8
ATTN_KV_HEADS = 2
ATTN_GROUP = ATTN_Q_HEADS // ATTN_KV_HEADS
ATTN_Q_DIM = ATTN_Q_HEADS * HEAD_DIM
ATTN_KV_DIM = ATTN_KV_HEADS * HEAD_DIM
ATTN_BLOCK = 128
ROPE_THETA = 10000.0

GDN_K_HEADS = 8
GDN_V_HEADS = 16
GDN_QK_DIM = GDN_K_HEADS * HEAD_DIM
GDN_V_DIM = GDN_V_HEADS * HEAD_DIM
GDN_CONV_DIM = 2 * GDN_QK_DIM + GDN_V_DIM
CONV_WIDTH = 5
CONV_PAD = CONV_WIDTH // 2
CHUNK = 64
GATE_LANES = 128
HALO = 16

VMEM_LIMIT_BYTES = 56 * 1024 * 1024

TM_PROJ = 256
TQ_ATTN = 512
TB_SCAN = 256
FF_CHUNK = 512
CONV_COLS = 512


def _rms(x, w):
    return x * lax.rsqrt(jnp.mean(x * x, axis=-1, keepdims=True) + RMS_EPS) * w


def _silu(x):
    return x * jax.nn.sigmoid(x)


def _dot(a, b):
    return jnp.dot(a, b, preferred_element_type=F32)


def _dot_nt(a, b):
    return lax.dot_general(a, b, (((1,), (1,)), ((), ())), preferred_element_type=F32)


def _dot_tn(a, b):
    return lax.dot_general(a, b, (((0,), (0,)), ((), ())), preferred_element_type=F32)


def _resident(shape):
    zeros = (0,) * len(shape)
    return pl.BlockSpec(shape, lambda *_: zeros, pipeline_mode=pl.Buffered(1))


def _params(n_axes):
    return pltpu.CompilerParams(dimension_semantics=("arbitrary",) * n_axes,
                                vmem_limit_bytes=VMEM_LIMIT_BYTES)


def _out_mlp_kernel(a_ref, x_ref, wo_ref, g_ref, wu_ref, wd_ref, o_ref):
    h = _dot(a_ref[...], wo_ref[...])
    x1 = x_ref[...] + _rms(h, g_ref[1:2, :])
    hn = _rms(x1, g_ref[2:3, :]).astype(BF16)
    acc = jnp.zeros(x1.shape, F32)
    for c in range(D_FF // FF_CHUNK):
        cols = slice(c * FF_CHUNK, (c + 1) * FF_CHUNK)
        up = _dot(hn, wu_ref[:, cols])
        act = jnp.square(jnp.maximum(up, 0.0)).astype(BF16)
        acc = acc + _dot(act, wd_ref[cols, :])
    o_ref[...] = x1 + _rms(acc, g_ref[3:4, :])


def _out_mlp(a, x, w_out, gains, w_up, w_down):
    n, k = a.shape
    tm = min(TM_PROJ, n)
    return pl.pallas_call(
        _out_mlp_kernel,
        out_shape=jax.ShapeDtypeStruct((n, D_MODEL), F32),
        grid=(n // tm,),
        in_specs=[
            pl.BlockSpec((tm, k), lambda i: (i, 0)),
            pl.BlockSpec((tm, D_MODEL), lambda i: (i, 0)),
            _resident((k, D_MODEL)),
            _resident((4, D_MODEL)),
            _resident((D_MODEL, D_FF)),
            _resident((D_FF, D_MODEL)),
        ],
        out_specs=pl.BlockSpec((tm, D_MODEL), lambda i: (i, 0)),
        compiler_params=_params(1),
        name="out_mlp",
    )(a, x, w_out, gains, w_up, w_down)


def _attn_in_kernel(x_ref, g_ref, w_ref, cos_ref, sin_ref, q_ref, k_ref, v_ref):
    hn = _rms(x_ref[...], g_ref[0:1, :]).astype(BF16)
    p = _dot(hn, w_ref[...])
    cos = cos_ref[...]
    sin = sin_ref[...]

    def rope(t):
        return t * cos + pltpu.roll(t, HEAD_DIM // 2, 1) * sin

    for h in range(ATTN_Q_HEADS):
        cols = slice(h * HEAD_DIM, (h + 1) * HEAD_DIM)
        q_ref[:, cols] = rope(p[:, cols]).astype(BF16)
    for h in range(ATTN_KV_HEADS):
        cols = slice(h * HEAD_DIM, (h + 1) * HEAD_DIM)
        src = slice(ATTN_Q_DIM + h * HEAD_DIM, ATTN_Q_DIM + (h + 1) * HEAD_DIM)
        k_ref[:, cols] = rope(p[:, src]).astype(BF16)
    v_ref[...] = p[:, ATTN_Q_DIM + ATTN_KV_DIM:].astype(BF16)


def _attn_in(x, gains, w_in, cos, sin, seq):
    n = x.shape[0]
    tm = min(TM_PROJ, seq)
    per_seq = seq // tm
    return pl.pallas_call(
        _attn_in_kernel,
        out_shape=(jax.ShapeDtypeStruct((n, ATTN_Q_DIM), BF16),
                   jax.ShapeDtypeStruct((n, ATTN_KV_DIM), BF16),
                   jax.ShapeDtypeStruct((n, ATTN_KV_DIM), BF16)),
        grid=(n // tm,),
        in_specs=[
            pl.BlockSpec((tm, D_MODEL), lambda i: (i, 0)),
            _resident((4, D_MODEL)),
            _resident(w_in.shape),
            pl.BlockSpec((tm, HEAD_DIM), lambda i: (i % per_seq, 0)),
            pl.BlockSpec((tm, HEAD_DIM), lambda i: (i % per_seq, 0)),
        ],
        out_specs=(pl.BlockSpec((tm, ATTN_Q_DIM), lambda i: (i, 0)),
                   pl.BlockSpec((tm, ATTN_KV_DIM), lambda i: (i, 0)),
                   pl.BlockSpec((tm, ATTN_KV_DIM), lambda i: (i, 0))),
        compiler_params=_params(1),
        name="attn_in",
    )(x, gains, w_in, cos, sin)


def _attn_kernel(sink_ref, q_ref, kp_ref, km_ref, kn_ref, vp_ref, vm_ref, vn_ref, o_ref, *, tq, seq):
    t = pl.program_id(0)
    p0 = (t * tq) % seq
    prev_bad = p0 == 0
    next_bad = (p0 + tq) == seq
    kfull = jnp.concatenate([kp_ref[...], km_ref[...], kn_ref[...]], axis=0)
    vfull = jnp.concatenate([vp_ref[...], vm_ref[...], vn_ref[...]], axis=0)

    rows = ATTN_GROUP * ATTN_BLOCK
    cols = 3 * ATTN_BLOCK
    r = lax.broadcasted_iota(jnp.int32, (rows, cols), 0) % ATTN_BLOCK
    cc = lax.broadcasted_iota(jnp.int32, (rows, cols), 1)
    cb = cc // ATTN_BLOCK
    c = cc % ATTN_BLOCK
    band_ok = ((cb == 0) & (c >= r)) | (cb == 1) | ((cb == 2) & (c <= r))
    scale = HEAD_DIM ** -0.5
    nblk = tq // ATTN_BLOCK
    for jb in range(nblk):
        ok = band_ok
        if jb == 0:
            ok = ok & (cc >= jnp.where(prev_bad, ATTN_BLOCK, 0))
        if jb == nblk - 1:
            ok = ok & (cc < jnp.where(next_bad, 2 * ATTN_BLOCK, cols))
        qrows = slice(jb * ATTN_BLOCK, (jb + 1) * ATTN_BLOCK)
        krows = slice(jb * ATTN_BLOCK, jb * ATTN_BLOCK + cols)
        for h in range(ATTN_KV_HEADS):
            hc = slice(h * HEAD_DIM, (h + 1) * HEAD_DIM)
            qs = jnp.concatenate(
                [q_ref[qrows, (h * ATTN_GROUP + g) * HEAD_DIM:(h * ATTN_GROUP + g + 1) * HEAD_DIM]
                 for g in range(ATTN_GROUP)], axis=0)
            s = _dot_nt(qs, kfull[krows, hc]) * scale
            s = jnp.where(ok, s, -jnp.inf)
            sink = jnp.concatenate(
                [jnp.full((ATTN_BLOCK, 1), sink_ref[h * ATTN_GROUP + g], F32) for g in range(ATTN_GROUP)],
                axis=0)
            m = jnp.maximum(jnp.max(s, axis=-1, keepdims=True), sink)
            p = jnp.exp(s - m)
            denom = jnp.sum(p, axis=-1, keepdims=True) + jnp.exp(sink - m)
            pn = (p / denom).astype(BF16)
            o = _dot(pn, vfull[krows, hc])
            for g in range(ATTN_GROUP):
                oc = slice((h * ATTN_GROUP + g) * HEAD_DIM, (h * ATTN_GROUP + g + 1) * HEAD_DIM)
                o_ref[qrows, oc] = o[g * ATTN_BLOCK:(g + 1) * ATTN_BLOCK, :].astype(BF16)


def _attn(q, k, v, sinks, seq):
    n = q.shape[0]
    tq = min(TQ_ATTN, seq)
    per = tq // ATTN_BLOCK
    last = n // ATTN_BLOCK - 1
    prev_map = lambda t: (jnp.maximum(t * per - 1, 0), 0)
    next_map = lambda t: (jnp.minimum((t + 1) * per, last), 0)
    kv_main = pl.BlockSpec((tq, ATTN_KV_DIM), lambda t: (t, 0))
    kv_prev = pl.BlockSpec((ATTN_BLOCK, ATTN_KV_DIM), prev_map)
    kv_next = pl.BlockSpec((ATTN_BLOCK, ATTN_KV_DIM), next_map)
    return pl.pallas_call(
        functools.partial(_attn_kernel, tq=tq, seq=seq),
        out_shape=jax.ShapeDtypeStruct((n, ATTN_Q_DIM), BF16),
        grid=(n // tq,),
        in_specs=[
            pl.BlockSpec(memory_space=pltpu.SMEM),
            pl.BlockSpec((tq, ATTN_Q_DIM), lambda t: (t, 0)),
            kv_prev, kv_main, kv_next, kv_prev, kv_main, kv_next,
        ],
        out_specs=pl.BlockSpec((tq, ATTN_Q_DIM), lambda t: (t, 0)),
        compiler_params=_params(1),
        name="attn_core",
    )(sinks, q, k, k, k, v, v, v)


def _gdn_in_kernel(xp_ref, xm_ref, xn_ref, g_ref, wc_ref, wz_ref, wg_ref, cw_ref, gp_ref,
                   q_ref, k_ref, v_ref, gz_ref, gc_ref, gr_ref, *, tm, seq):
    i = pl.program_id(0)
    p0 = (i * tm) % seq
    gain = g_ref[0:1, :]
    xp = jnp.where(p0 == 0, 0.0, xp_ref[...])
    xn = jnp.where(p0 + tm == seq, 0.0, xn_ref[...])
    hn = _rms(jnp.concatenate([xp, xm_ref[...], xn], axis=0), gain).astype(BF16)
    hm = hn[HALO:HALO + tm, :]

    qk_scale = HEAD_DIM ** -0.5
    for c in range(GDN_CONV_DIM // CONV_COLS):
        cols = slice(c * CONV_COLS, (c + 1) * CONV_COLS)
        pc = _dot(hn, wc_ref[:, cols])
        acc = None
        for j in range(CONV_WIDTH):
            lo = HALO - CONV_PAD + j
            term = pc[lo:lo + tm, :] * cw_ref[j:j + 1, cols]
            acc = term if acc is None else acc + term
        act = _silu(acc)
        for hh in range(CONV_COLS // HEAD_DIM):
            col0 = c * CONV_COLS + hh * HEAD_DIM
            t = act[:, hh * HEAD_DIM:(hh + 1) * HEAD_DIM]
            if col0 < 2 * GDN_QK_DIM:
                t = t * lax.rsqrt(jnp.sum(t * t, axis=-1, keepdims=True) + RMS_EPS)
            if col0 < GDN_QK_DIM:
                q_ref[:, col0:col0 + HEAD_DIM] = t * qk_scale
            elif col0 < 2 * GDN_QK_DIM:
                k_ref[:, col0 - GDN_QK_DIM:col0 - GDN_QK_DIM + HEAD_DIM] = t
            else:
                v_ref[:, col0 - 2 * GDN_QK_DIM:col0 - 2 * GDN_QK_DIM + HEAD_DIM] = t

    for c in range(GDN_V_DIM // CONV_COLS):
        cols = slice(c * CONV_COLS, (c + 1) * CONV_COLS)
        gz_ref[:, cols] = _silu(_dot(hm, wz_ref[:, cols]))

    ga = _dot(hm, wg_ref[...])
    lane = lax.broadcasted_iota(jnp.int32, (1, GATE_LANES), 1)
    is_decay = (lane & 2) != 0
    is_bwd = (lane & 4) != 0
    beta = jax.nn.sigmoid(ga)
    pre = ga + gp_ref[1:2, :]
    softplus = jnp.maximum(pre, 0.0) + jnp.log1p(jnp.exp(-jnp.abs(pre)))
    gval = jnp.where(is_decay, -jnp.exp(gp_ref[0:1, :]) * softplus, 0.0)
    rr = lax.broadcasted_iota(jnp.int32, (tm, tm), 0)
    cc = lax.broadcasted_iota(jnp.int32, (tm, tm), 1)
    same = (rr // CHUNK) == (cc // CHUNK)
    tri_f = (same & (rr >= cc)).astype(F32)
    tri_b = (same & (rr <= cc)).astype(F32)
    cum_f = jnp.dot(tri_f, gval, preferred_element_type=F32, precision=lax.Precision.HIGHEST)
    cum_b = jnp.dot(tri_b, gval, preferred_element_type=F32, precision=lax.Precision.HIGHEST)
    gates = jnp.where(is_decay, jnp.where(is_bwd, cum_b, cum_f), beta)
    gates_t = gates.T
    for kh in range(GDN_K_HEADS):
        gc_ref[kh] = gates[:, 8 * kh:8 * kh + 8]
        gr_ref[kh] = gates_t[8 * kh:8 * kh + 8, :]


def _gdn_in(x, gains, w_conv, w_z, w_gate, conv_w, gate_params, seq):
    n = x.shape[0]
    tm = min(TM_PROJ, seq)
    per = tm // HALO
    last = n // HALO - 1
    return pl.pallas_call(
        functools.partial(_gdn_in_kernel, tm=tm, seq=seq),
        out_shape=(jax.ShapeDtypeStruct((n, GDN_QK_DIM), F32),
                   jax.ShapeDtypeStruct((n, GDN_QK_DIM), F32),
                   jax.ShapeDtypeStruct((n, GDN_V_DIM), F32),
                   jax.ShapeDtypeStruct((n, GDN_V_DIM), F32),
                   jax.ShapeDtypeStruct((GDN_K_HEADS, n, 8), F32),
                   jax.ShapeDtypeStruct((GDN_K_HEADS, 8, n), F32)),
        grid=(n // tm,),
        in_specs=[
            pl.BlockSpec((HALO, D_MODEL), lambda i: (jnp.maximum(i * per - 1, 0), 0)),
            pl.BlockSpec((tm, D_MODEL), lambda i: (i, 0)),
            pl.BlockSpec((HALO, D_MODEL), lambda i: (jnp.minimum((i + 1) * per, last), 0)),
            _resident((4, D_MODEL)),
            _resident(w_conv.shape),
            _resident(w_z.shape),
            _resident(w_gate.shape),
            _resident(conv_w.shape),
            _resident(gate_params.shape),
        ],
        out_specs=(pl.BlockSpec((tm, GDN_QK_DIM), lambda i: (i, 0)),
                   pl.BlockSpec((tm, GDN_QK_DIM), lambda i: (i, 0)),
                   pl.BlockSpec((tm, GDN_V_DIM), lambda i: (i, 0)),
                   pl.BlockSpec((tm, GDN_V_DIM), lambda i: (i, 0)),
                   pl.BlockSpec((GDN_K_HEADS, tm, 8), lambda i: (0, i, 0)),
                   pl.BlockSpec((GDN_K_HEADS, 8, tm), lambda i: (0, 0, i))),
        compiler_params=_params(1),
        name="gdn_in",
    )(x, x, x, gains, w_conv, w_z, w_gate, conv_w, gate_params)


def _unit_tri_inverse_minus_eye(l_mat, r, c):
    e = -jnp.where((r // 2) == (c // 2), l_mat, 0.0)
    b = 2
    while b < CHUNK:
        cb = jnp.where(((r // (2 * b)) == (c // (2 * b))) & ((r // b) != (c // b)), l_mat, 0.0)
        z = cb + _dot(cb.astype(BF16), e.astype(BF16))
        y = z + _dot(e.astype(BF16), z.astype(BF16))
        e = e - y
        b *= 2
    return e


def _gdn_scan_kernel(*refs, tb, reverse, final):
    if final:
        (q_ref, k_ref, v_ref, gc_ref, gr_ref, of_ref, gz_ref, nw_ref, o_ref, s_ref) = refs
    else:
        (q_ref, k_ref, v_ref, gc_ref, gr_ref, o_ref, s_ref) = refs
    i = pl.program_id(1)
    kh = pl.program_id(2)

    @pl.when(i == 0)
    def _():
        s_ref[2 * kh] = jnp.zeros((HEAD_DIM, HEAD_DIM), F32)
        s_ref[2 * kh + 1] = jnp.zeros((HEAD_DIM, HEAD_DIM), F32)

    r = lax.broadcasted_iota(jnp.int32, (CHUNK, CHUNK), 0)
    c = lax.broadcasted_iota(jnp.int32, (CHUNK, CHUNK), 1)
    incl = (r <= c) if reverse else (r >= c)
    strict = (r < c) if reverse else (r > c)
    base = 4 if reverse else 0
    last = 0 if reverse else CHUNK - 1

    states = [s_ref[2 * kh], s_ref[2 * kh + 1]]
    nchunks = tb // CHUNK
    order = range(nchunks - 1, -1, -1) if reverse else range(nchunks)
    for ci in order:
        rows = slice(ci * CHUNK, (ci + 1) * CHUNK)
        qc = q_ref[rows, :]
        kc = k_ref[rows, :]
        kc16 = kc.astype(BF16)
        a_all = _dot_nt(jnp.concatenate([qc.astype(BF16), kc16], axis=0), kc16)
        a_qk = a_all[:CHUNK]
        a_kk = a_all[CHUNK:]
        gcol = gc_ref[0, rows, :]
        grow = gr_ref[0, :, rows]
        for a in range(2):
            beta_c = gcol[:, base + a:base + a + 1]
            g_c = gcol[:, base + 2 + a:base + 3 + a]
            g_r = grow[base + 2 + a:base + 3 + a, :]
            g_last = g_r[:, last:last + 1]
            decay = jnp.exp(jnp.where(incl, g_c - g_r, -jnp.inf))
            l_mat = jnp.where(strict, a_kk * decay, 0.0) * beta_c
            e = _unit_tri_inverse_minus_eye(l_mat, r, c)
            vc = v_ref[rows, a * HEAD_DIM:(a + 1) * HEAD_DIM]
            eg = jnp.exp(g_c)
            rhs = jnp.concatenate([vc * beta_c, kc * (beta_c * eg)], axis=1)
            uw = rhs + _dot(e.astype(BF16), rhs.astype(BF16))
            u = uw[:, :HEAD_DIM]
            w = uw[:, HEAD_DIM:]
            s16 = states[a].astype(BF16)
            ws_qs = _dot(jnp.concatenate([w, qc * eg], axis=0).astype(BF16), s16)
            v_new = u - ws_qs[:CHUNK]
            v_new16 = v_new.astype(BF16)
            o = ws_qs[CHUNK:] + _dot((a_qk * decay).astype(BF16), v_new16)
            kg = kc * jnp.exp(g_last - g_c)
            states[a] = states[a] * jnp.exp(g_last) + _dot_tn(kg.astype(BF16), v_new16)
            ocols = slice(a * HEAD_DIM, (a + 1) * HEAD_DIM)
            if final:
                tot = o + of_ref[rows, ocols]
                y = _rms(tot, nw_ref[...]) * gz_ref[rows, ocols]
                o_ref[rows, ocols] = y.astype(BF16)
            else:
                o_ref[rows, ocols] = o
    s_ref[2 * kh] = states[0]
    s_ref[2 * kh + 1] = states[1]


def _gdn_scan(q, k, v, gc, gr, seq, *, reverse, o_fwd=None, gz=None, norm_w=None):
    n = q.shape[0]
    final = o_fwd is not None
    tb = min(TB_SCAN, seq)
    nblk = seq // tb
    batch = n // seq

    def row(b, i):
        return b * nblk + ((nblk - 1 - i) if reverse else i)

    qk_spec = pl.BlockSpec((tb, HEAD_DIM), lambda b, i, h: (row(b, i), h))
    v_spec = pl.BlockSpec((tb, 2 * HEAD_DIM), lambda b, i, h: (row(b, i), h))
    in_specs = [qk_spec, qk_spec, v_spec,
                pl.BlockSpec((1, tb, 8), lambda b, i, h: (h, row(b, i), 0)),
                pl.BlockSpec((1, 8, tb), lambda b, i, h: (h, 0, row(b, i)))]
    args = [q, k, v, gc, gr]
    if final:
        in_specs += [v_spec, v_spec, _resident((1, HEAD_DIM))]
        args += [o_fwd, gz, norm_w]
    return pl.pallas_call(
        functools.partial(_gdn_scan_kernel, tb=tb, reverse=reverse, final=final),
        out_shape=jax.ShapeDtypeStruct((n, GDN_V_DIM), BF16 if final else F32),
        grid=(batch, nblk, GDN_K_HEADS),
        in_specs=in_specs,
        out_specs=v_spec,
        scratch_shapes=[pltpu.VMEM((GDN_V_HEADS, HEAD_DIM, HEAD_DIM), F32)],
        compiler_params=_params(3),
        name="gdn_scan_bwd" if reverse else "gdn_scan_fwd",
    )(*args)


def _rope_tables(seq):
    half = HEAD_DIM // 2
    inv_freq = 1.0 / (ROPE_THETA ** (jnp.arange(half, dtype=F32) / half))
    ang = jnp.arange(seq, dtype=F32)[:, None] * inv_freq[None, :]
    cos = jnp.cos(ang)
    sin = jnp.sin(ang)
    return jnp.concatenate([cos, cos], axis=-1), jnp.concatenate([-sin, sin], axis=-1)


def _gate_lane_order():
    kh = jnp.arange(GDN_K_HEADS)[:, None, None, None]
    d = jnp.arange(2)[None, :, None, None]
    kind = jnp.arange(2)[None, None, :, None]
    a = jnp.arange(2)[None, None, None, :]
    return (d * 2 * GDN_V_HEADS + kind * GDN_V_HEADS + 2 * kh + a).reshape(-1)


def _gate_vector(p):
    t = p.astype(F32).reshape(2, GDN_K_HEADS, 2).transpose(1, 0, 2)
    t = jnp.stack([jnp.zeros_like(t), t], axis=2)
    return jnp.pad(t.reshape(-1), (0, GATE_LANES - 4 * GDN_V_HEADS))


def _trunk(x, seq, norm_gains, attn_w_in, attn_sinks, attn_w_out, gdn_w_in, gdn_conv_w, gdn_A_log,
           gdn_dt_bias, gdn_norm_w, gdn_w_out, mlp_w_up, mlp_w_down):
    depth = norm_gains.shape[0]
    cos, sin = _rope_tables(seq)
    lane_order = _gate_lane_order()
    for i in range(depth):
        j = i // 2
        gains = norm_gains[i].astype(F32)
        if i % 2 == 0:
            q, k, v = _attn_in(x, gains, attn_w_in[j].astype(BF16), cos, sin, seq)
            mixed = _attn(q, k, v, attn_sinks[j].astype(F32), seq)
            w_out = attn_w_out[j]
        else:
            w = gdn_w_in[j]
            w_gate = w[:, GDN_CONV_DIM + GDN_V_DIM:][:, lane_order]
            w_gate = jnp.pad(w_gate, ((0, 0), (0, GATE_LANES - w_gate.shape[1])))
            gate_params = jnp.stack([_gate_vector(gdn_A_log[j]), _gate_vector(gdn_dt_bias[j])])
            q, k, v, gz, gc, gr = _gdn_in(
                x, gains, w[:, :GDN_CONV_DIM].astype(BF16),
                w[:, GDN_CONV_DIM:GDN_CONV_DIM + GDN_V_DIM].astype(BF16), w_gate.astype(BF16),
                gdn_conv_w[j].astype(F32), gate_params, seq)
            o_fwd = _gdn_scan(q, k, v, gc, gr, seq, reverse=False)
            mixed = _gdn_scan(q, k, v, gc, gr, seq, reverse=True, o_fwd=o_fwd, gz=gz,
                              norm_w=gdn_norm_w[j].astype(F32).reshape(1, HEAD_DIM))
            w_out = gdn_w_out[j]
        x = _out_mlp(mixed, x, w_out.astype(BF16), gains, mlp_w_up[i].astype(BF16),
                     mlp_w_down[i].astype(BF16))
    return x


def kernel(x_prompt, x_sample, norm_gains, attn_w_in, attn_sinks, attn_w_out, gdn_w_in, gdn_conv_w,
           gdn_A_log, gdn_dt_bias, gdn_norm_w, gdn_w_out, mlp_w_up, mlp_w_down):
    seq = x_prompt.shape[1]
    assert x_sample.shape[1] == seq and x_prompt.shape[2] == D_MODEL
    x = jnp.concatenate([x_prompt.reshape(-1, D_MODEL), x_sample.reshape(-1, D_MODEL)], axis=0)
    y = _trunk(x, seq, norm_gains, attn_w_in, attn_sinks, attn_w_out, gdn_w_in, gdn_conv_w,
               gdn_A_log, gdn_dt_bias, gdn_norm_w, gdn_w_out, mlp_w_up, mlp_w_down)
    n_prompt = x_prompt.shape[0] * seq
    return (y[:n_prompt].reshape(x_prompt.shape), y[n_prompt:].reshape(x_sample.shape))
```

```python
import functools

import jax
import jax.numpy as jnp
from jax import lax
from jax.experimental import pallas as pl
from jax.experimental.pallas import tpu as pltpu

F32 = jnp.float32
BF16 = jnp.bfloat16

D_MODEL = 1024
D_FF = 4 * D_MODEL
RMS_EPS = 1e-6
HEAD_DIM = 128

ATTN_Q_HEADS = 8
ATTN_KV_HEADS = 2
ATTN_GROUP = ATTN_Q_HEADS // ATTN_KV_HEADS
ATTN_Q_DIM = ATTN_Q_HEADS * HEAD_DIM
ATTN_KV_DIM = ATTN_KV_HEADS * HEAD_DIM
ATTN_BLOCK = 128
ROPE_THETA = 10000.0

GDN_K_HEADS = 8
GDN_V_HEADS = 16
GDN_QK_DIM = GDN_K_HEADS * HEAD_DIM
GDN_V_DIM = GDN_V_HEADS * HEAD_DIM
GDN_CONV_DIM = 2 * GDN_QK_DIM + GDN_V_DIM
CONV_WIDTH = 5
CONV_PAD = CONV_WIDTH // 2
CHUNK = 64
GATE_LANES = 128
HALO = 16

VMEM_LIMIT_BYTES = 56 * 1024 * 1024

TM_PROJ = 256
TQ_ATTN = 512
TB_SCAN = 128
FF_CHUNK = 512
CONV_COLS = 512


def _rms(x, w):
    return x * lax.rsqrt(jnp.mean(x * x, axis=-1, keepdims=True) + RMS_EPS) * w


def _silu(x):
    return x * jax.nn.sigmoid(x)


def _dot(a, b):
    return jnp.dot(a, b, preferred_element_type=F32)


def _dot_nt(a, b):
    return lax.dot_general(a, b, (((1,), (1,)), ((), ())), preferred_element_type=F32)


def _bdot(a, b):
    return lax.dot_general(a, b, (((2,), (1,)), ((0,), (0,))), preferred_element_type=F32)


def _bdot_nt(a, b):
    return lax.dot_general(a, b, (((2,), (2,)), ((0,), (0,))), preferred_element_type=F32)


def _resident(shape):
    zeros = (0,) * len(shape)
    return pl.BlockSpec(shape, lambda *_: zeros, pipeline_mode=pl.Buffered(1))


def _params(n_axes):
    return pltpu.CompilerParams(dimension_semantics=("arbitrary",) * n_axes,
                                vmem_limit_bytes=VMEM_LIMIT_BYTES)


def _out_mlp_kernel(a_ref, x_ref, wo_ref, g_ref, wu_ref, wd_ref, o_ref):
    h = _dot(a_ref[...], wo_ref[...])
    x1 = x_ref[...] + _rms(h, g_ref[1:2, :])
    hn = _rms(x1, g_ref[2:3, :]).astype(BF16)
    acc = jnp.zeros(x1.shape, F32)
    for c in range(D_FF // FF_CHUNK):
        cols = slice(c * FF_CHUNK, (c + 1) * FF_CHUNK)
        up = _dot(hn, wu_ref[:, cols])
        act = jnp.square(jnp.maximum(up, 0.0)).astype(BF16)
        acc = acc + _dot(act, wd_ref[cols, :])
    o_ref[...] = x1 + _rms(acc, g_ref[3:4, :])


def _out_mlp(a, x, w_out, gains, w_up, w_down):
    n, k = a.shape
    tm = min(TM_PROJ, n)
    return pl.pallas_call(
        _out_mlp_kernel,
        out_shape=jax.ShapeDtypeStruct((n, D_MODEL), F32),
        grid=(n // tm,),
        in_specs=[
            pl.BlockSpec((tm, k), lambda i: (i, 0)),
            pl.BlockSpec((tm, D_MODEL), lambda i: (i, 0)),
            _resident((k, D_MODEL)),
            _resident((4, D_MODEL)),
            _resident((D_MODEL, D_FF)),
            _resident((D_FF, D_MODEL)),
        ],
        out_specs=pl.BlockSpec((tm, D_MODEL), lambda i: (i, 0)),
        compiler_params=_params(1),
        name="out_mlp",
    )(a, x, w_out, gains, w_up, w_down)


def _attn_in_kernel(x_ref, g_ref, w_ref, cos_ref, sin_ref, q_ref, k_ref, v_ref):
    hn = _rms(x_ref[...], g_ref[0:1, :]).astype(BF16)
    p = _dot(hn, w_ref[...])
    cos = cos_ref[...]
    sin = sin_ref[...]

    def rope(t):
        return t * cos + pltpu.roll(t, HEAD_DIM // 2, 1) * sin

    for h in range(ATTN_Q_HEADS):
        cols = slice(h * HEAD_DIM, (h + 1) * HEAD_DIM)
        q_ref[:, cols] = rope(p[:, cols]).astype(BF16)
    for h in range(ATTN_KV_HEADS):
        cols = slice(h * HEAD_DIM, (h + 1) * HEAD_DIM)
        src = slice(ATTN_Q_DIM + h * HEAD_DIM, ATTN_Q_DIM + (h + 1) * HEAD_DIM)
        k_ref[:, cols] = rope(p[:, src]).astype(BF16)
    v_ref[...] = p[:, ATTN_Q_DIM + ATTN_KV_DIM:].astype(BF16)


def _attn_in(x, gains, w_in, cos, sin, seq):
    n = x.shape[0]
    tm = min(TM_PROJ, seq)
    per_seq = seq // tm
    return pl.pallas_call(
        _attn_in_kernel,
        out_shape=(jax.ShapeDtypeStruct((n, ATTN_Q_DIM), BF16),
                   jax.ShapeDtypeStruct((n, ATTN_KV_DIM), BF16),
                   jax.ShapeDtypeStruct((n, ATTN_KV_DIM), BF16)),
        grid=(n // tm,),
        in_specs=[
            pl.BlockSpec((tm, D_MODEL), lambda i: (i, 0)),
            _resident((4, D_MODEL)),
            _resident(w_in.shape),
            pl.BlockSpec((tm, HEAD_DIM), lambda i: (i % per_seq, 0)),
            pl.BlockSpec((tm, HEAD_DIM), lambda i: (i % per_seq, 0)),
        ],
        out_specs=(pl.BlockSpec((tm, ATTN_Q_DIM), lambda i: (i, 0)),
                   pl.BlockSpec((tm, ATTN_KV_DIM), lambda i: (i, 0)),
                   pl.BlockSpec((tm, ATTN_KV_DIM), lambda i: (i, 0))),
        compiler_params=_params(1),
        name="attn_in",
    )(x, gains, w_in, cos, sin)


def _attn_kernel(sink_ref, q_ref, kp_ref, km_ref, kn_ref, vp_ref, vm_ref, vn_ref, o_ref, *, tq, seq):
    t = pl.program_id(0)
    p0 = (t * tq) % seq
    prev_bad = p0 == 0
    next_bad = (p0 + tq) == seq
    kfull = jnp.concatenate([kp_ref[...], km_ref[...], kn_ref[...]], axis=0)
    vfull = jnp.concatenate([vp_ref[...], vm_ref[...], vn_ref[...]], axis=0)

    rows = ATTN_GROUP * ATTN_BLOCK
    cols = 3 * ATTN_BLOCK
    r = lax.broadcasted_iota(jnp.int32, (rows, cols), 0) % ATTN_BLOCK
    cc = lax.broadcasted_iota(jnp.int32, (rows, cols), 1)
    cb = cc // ATTN_BLOCK
    c = cc % ATTN_BLOCK
    band_ok = ((cb == 0) & (c >= r)) | (cb == 1) | ((cb == 2) & (c <= r))
    scale = HEAD_DIM ** -0.5
    nblk = tq // ATTN_BLOCK
    for jb in range(nblk):
        ok = band_ok
        if jb == 0:
            ok = ok & (cc >= jnp.where(prev_bad, ATTN_BLOCK, 0))
        if jb == nblk - 1:
            ok = ok & (cc < jnp.where(next_bad, 2 * ATTN_BLOCK, cols))
        qrows = slice(jb * ATTN_BLOCK, (jb + 1) * ATTN_BLOCK)
        krows = slice(jb * ATTN_BLOCK, jb * ATTN_BLOCK + cols)
        for h in range(ATTN_KV_HEADS):
            hc = slice(h * HEAD_DIM, (h + 1) * HEAD_DIM)
            qs = jnp.concatenate(
                [q_ref[qrows, (h * ATTN_GROUP + g) * HEAD_DIM:(h * ATTN_GROUP + g + 1) * HEAD_DIM]
                 for g in range(ATTN_GROUP)], axis=0)
            s = _dot_nt(qs, kfull[krows, hc]) * scale
            s = jnp.where(ok, s, -jnp.inf)
            sink = jnp.concatenate(
                [jnp.full((ATTN_BLOCK, 1), sink_ref[h * ATTN_GROUP + g], F32) for g in range(ATTN_GROUP)],
                axis=0)
            m = jnp.maximum(jnp.max(s, axis=-1, keepdims=True), sink)
            p = jnp.exp(s - m)
            denom = jnp.sum(p, axis=-1, keepdims=True) + jnp.exp(sink - m)
            pn = (p / denom).astype(BF16)
            o = _dot(pn, vfull[krows, hc])
            for g in range(ATTN_GROUP):
                oc = slice((h * ATTN_GROUP + g) * HEAD_DIM, (h * ATTN_GROUP + g + 1) * HEAD_DIM)
                o_ref[qrows, oc] = o[g * ATTN_BLOCK:(g + 1) * ATTN_BLOCK, :].astype(BF16)


def _attn(q, k, v, sinks, seq):
    n = q.shape[0]
    tq = min(TQ_ATTN, seq)
    per = tq // ATTN_BLOCK
    last = n // ATTN_BLOCK - 1
    prev_map = lambda t: (jnp.maximum(t * per - 1, 0), 0)
    next_map = lambda t: (jnp.minimum((t + 1) * per, last), 0)
    kv_main = pl.BlockSpec((tq, ATTN_KV_DIM), lambda t: (t, 0))
    kv_prev = pl.BlockSpec((ATTN_BLOCK, ATTN_KV_DIM), prev_map)
    kv_next = pl.BlockSpec((ATTN_BLOCK, ATTN_KV_DIM), next_map)
    return pl.pallas_call(
        functools.partial(_attn_kernel, tq=tq, seq=seq),
        out_shape=jax.ShapeDtypeStruct((n, ATTN_Q_DIM), BF16),
        grid=(n // tq,),
        in_specs=[
            pl.BlockSpec(memory_space=pltpu.SMEM),
            pl.BlockSpec((tq, ATTN_Q_DIM), lambda t: (t, 0)),
            kv_prev, kv_main, kv_next, kv_prev, kv_main, kv_next,
        ],
        out_specs=pl.BlockSpec((tq, ATTN_Q_DIM), lambda t: (t, 0)),
        compiler_params=_params(1),
        name="attn_core",
    )(sinks, q, k, k, k, v, v, v)


def _gdn_in_kernel(xp_ref, xm_ref, xn_ref, g_ref, wc_ref, wz_ref, wg_ref, cw_ref, gp_ref,
                   q_ref, k_ref, kt_ref, v_ref, gz_ref, gc_ref, gr_ref, *, tm, seq):
    i = pl.program_id(0)
    p0 = (i * tm) % seq
    gain = g_ref[0:1, :]
    xp = jnp.where(p0 == 0, 0.0, xp_ref[...])
    xn = jnp.where(p0 + tm == seq, 0.0, xn_ref[...])
    hn = _rms(jnp.concatenate([xp, xm_ref[...], xn], axis=0), gain).astype(BF16)
    hm = hn[HALO:HALO + tm, :]

    qk_scale = HEAD_DIM ** -0.5
    for c in range(GDN_CONV_DIM // CONV_COLS):
        cols = slice(c * CONV_COLS, (c + 1) * CONV_COLS)
        pc = _dot(hn, wc_ref[:, cols])
        acc = None
        for j in range(CONV_WIDTH):
            lo = HALO - CONV_PAD + j
            term = pc[lo:lo + tm, :] * cw_ref[j:j + 1, cols]
            acc = term if acc is None else acc + term
        act = _silu(acc)
        for hh in range(CONV_COLS // HEAD_DIM):
            col0 = c * CONV_COLS + hh * HEAD_DIM
            t = act[:, hh * HEAD_DIM:(hh + 1) * HEAD_DIM]
            if col0 < 2 * GDN_QK_DIM:
                t = t * lax.rsqrt(jnp.sum(t * t, axis=-1, keepdims=True) + RMS_EPS)
            if col0 < GDN_QK_DIM:
                q_ref[:, col0:col0 + HEAD_DIM] = (t * qk_scale).astype(BF16)
            elif col0 < 2 * GDN_QK_DIM:
                kc0 = col0 - GDN_QK_DIM
                k_ref[:, kc0:kc0 + HEAD_DIM] = t.astype(BF16)
                kt_ref[0, kc0:kc0 + HEAD_DIM, :] = t.T
            else:
                vc0 = col0 - 2 * GDN_QK_DIM
                v_ref[:, vc0:vc0 + HEAD_DIM] = t.astype(BF16)

    for c in range(GDN_V_DIM // CONV_COLS):
        cols = slice(c * CONV_COLS, (c + 1) * CONV_COLS)
        gz_ref[:, cols] = _silu(_dot(hm, wz_ref[:, cols]))

    ga = _dot(hm, wg_ref[...])
    lane = lax.broadcasted_iota(jnp.int32, (1, GATE_LANES), 1)
    is_decay = (lane & 1) != 0
    is_bwd = (lane & 2) != 0
    beta = jax.nn.sigmoid(ga)
    pre = ga + gp_ref[1:2, :]
    softplus = jnp.maximum(pre, 0.0) + jnp.log1p(jnp.exp(-jnp.abs(pre)))
    gval = jnp.where(is_decay, -jnp.exp(gp_ref[0:1, :]) * softplus, 0.0)
    rr = lax.broadcasted_iota(jnp.int32, (tm, tm), 0)
    cc = lax.broadcasted_iota(jnp.int32, (tm, tm), 1)
    same = (rr // CHUNK) == (cc // CHUNK)
    tri_f = (same & (rr >= cc)).astype(F32)
    tri_b = (same & (rr <= cc)).astype(F32)
    cum_f = jnp.dot(tri_f, gval, preferred_element_type=F32, precision=lax.Precision.HIGHEST)
    cum_b = jnp.dot(tri_b, gval, preferred_element_type=F32, precision=lax.Precision.HIGHEST)
    gates = jnp.where(is_decay, jnp.where(is_bwd, cum_b, cum_f), beta)
    gc_ref[...] = gates
    gates_t = gates.T
    low_half = lax.broadcasted_iota(jnp.int32, (4, 2 * CHUNK), 1) < CHUNK
    for kh in range(GDN_K_HEADS):
        for m in range(tm // (2 * CHUNK)):
            tok = slice(m * 2 * CHUNK, (m + 1) * 2 * CHUNK)
            head_a = gates_t[8 * kh:8 * kh + 4, tok]
            head_b = gates_t[8 * kh + 4:8 * kh + 8, tok]
            gr_ref[0, kh, 2 * m] = jnp.where(low_half, head_a, pltpu.roll(head_b, CHUNK, 1))
            gr_ref[0, kh, 2 * m + 1] = jnp.where(low_half, pltpu.roll(head_a, CHUNK, 1), head_b)


def _gdn_in(x, gains, w_conv, w_z, w_gate, conv_w, gate_params, seq):
    n = x.shape[0]
    batch = n // seq
    tm = min(TM_PROJ, seq)
    per_seq = seq // tm
    per = tm // HALO
    last = n // HALO - 1
    row = lambda i: (i, 0)
    return pl.pallas_call(
        functools.partial(_gdn_in_kernel, tm=tm, seq=seq),
        out_shape=(jax.ShapeDtypeStruct((n, GDN_QK_DIM), BF16),
                   jax.ShapeDtypeStruct((n, GDN_QK_DIM), BF16),
                   jax.ShapeDtypeStruct((batch, GDN_QK_DIM, seq), F32),
                   jax.ShapeDtypeStruct((n, GDN_V_DIM), BF16),
                   jax.ShapeDtypeStruct((n, GDN_V_DIM), F32),
                   jax.ShapeDtypeStruct((n, GATE_LANES), F32),
                   jax.ShapeDtypeStruct((batch, GDN_K_HEADS, seq // CHUNK, 4, 2 * CHUNK), F32)),
        grid=(n // tm,),
        in_specs=[
            pl.BlockSpec((HALO, D_MODEL), lambda i: (jnp.maximum(i * per - 1, 0), 0)),
            pl.BlockSpec((tm, D_MODEL), row),
            pl.BlockSpec((HALO, D_MODEL), lambda i: (jnp.minimum((i + 1) * per, last), 0)),
            _resident((4, D_MODEL)),
            _resident(w_conv.shape),
            _resident(w_z.shape),
            _resident(w_gate.shape),
            _resident(conv_w.shape),
            _resident(gate_params.shape),
        ],
        out_specs=(pl.BlockSpec((tm, GDN_QK_DIM), row),
                   pl.BlockSpec((tm, GDN_QK_DIM), row),
                   pl.BlockSpec((1, GDN_QK_DIM, tm), lambda i: (i // per_seq, 0, i % per_seq)),
                   pl.BlockSpec((tm, GDN_V_DIM), row),
                   pl.BlockSpec((tm, GDN_V_DIM), row),
                   pl.BlockSpec((tm, GATE_LANES), row),
                   pl.BlockSpec((1, GDN_K_HEADS, tm // CHUNK, 4, 2 * CHUNK),
                                lambda i: (i // per_seq, 0, i % per_seq, 0, 0))),
        compiler_params=_params(1),
        name="gdn_in",
    )(x, x, x, gains, w_conv, w_z, w_gate, conv_w, gate_params)


def _pair_diag(x):
    half = x.shape[2] // 2
    low = lax.broadcasted_iota(jnp.int32, x.shape, 2) < half
    return jnp.concatenate([jnp.where(low, x, 0.0), jnp.where(low, 0.0, x)], axis=1)


def _lane_diag(x, zero):
    half = x.shape[2] // 2
    return jnp.concatenate([jnp.concatenate([x[:, :, :half], zero], axis=2),
                            jnp.concatenate([zero, x[:, :, half:]], axis=2)], axis=1)


def _pair_tri_inverse_minus_eye(l_mat, r, c):
    e = -jnp.where((r // 2) == (c // 2), l_mat, 0.0)
    b = 2
    while b < CHUNK:
        cb = jnp.where(((r // (2 * b)) == (c // (2 * b))) & ((r // b) != (c // b)), l_mat, 0.0)
        z = cb + _bdot(cb.astype(BF16), _pair_diag(e).astype(BF16))
        y = z + _bdot(e.astype(BF16), _pair_diag(z).astype(BF16))
        e = e - y
        b *= 2
    return e


def _gdn_scan_kernel(*refs, tb, nb, reverse, final):
    if final:
        (q_ref, k_ref, kt_ref, v_ref, gc_ref, gr_ref, of_ref, gz_ref, nw_ref, o_ref, s_ref) = refs
    else:
        (q_ref, k_ref, kt_ref, v_ref, gc_ref, gr_ref, o_ref, s_ref) = refs

    @pl.when(pl.program_id(0) == 0)
    def _():
        s_ref[...] = jnp.zeros(s_ref.shape, F32)

    nc = tb // CHUNK
    kh_n = GDN_K_HEADS
    index = [(b, ci, j) for b in range(nb) for ci in range(nc) for j in range(kh_n)]
    npair = len(index)
    d = 1 if reverse else 0
    last = 0 if reverse else CHUNK - 1

    def rows(ci):
        return slice(ci * CHUNK, (ci + 1) * CHUNK)

    def head(j, width=HEAD_DIM):
        return slice(j * width, (j + 1) * width)

    r = lax.broadcasted_iota(jnp.int32, (1, CHUNK, 2 * CHUNK), 1)
    lane = lax.broadcasted_iota(jnp.int32, (1, CHUNK, 2 * CHUNK), 2)
    c = lane % CHUNK
    low = lane < CHUNK
    incl = (r <= c) if reverse else (r >= c)
    strict = (r < c) if reverse else (r > c)
    eye = (r == c).astype(F32)

    q3 = jnp.stack([q_ref[b, rows(ci), head(j)] for b, ci, j in index])
    k3 = jnp.stack([k_ref[b, rows(ci), head(j)] for b, ci, j in index])
    v3 = jnp.stack([v_ref[b, rows(ci), head(j, 2 * HEAD_DIM)] for b, ci, j in index])
    prod = _bdot_nt(jnp.concatenate([q3, k3], axis=1), jnp.concatenate([k3, k3], axis=1))
    qk = prod[:, :CHUNK]
    kk = prod[:, CHUNK:]

    def g_column(b, ci, j):
        la = 8 * j + 2 * d + 1
        col_a = jnp.broadcast_to(gc_ref[b, rows(ci), la:la + 1], (CHUNK, 2 * CHUNK))
        col_b = jnp.broadcast_to(gc_ref[b, rows(ci), la + 4:la + 5], (CHUNK, 2 * CHUNK))
        return jnp.where(low[0], col_a, col_b)

    g_col = jnp.stack([g_column(b, ci, j) for b, ci, j in index])
    g_row = jnp.stack([gr_ref[b, j, ci, 2 * d + 1:2 * d + 2, :] for b, ci, j in index])
    b_row = jnp.stack([gr_ref[b, j, ci, 2 * d:2 * d + 1, :] for b, ci, j in index])

    decay = jnp.exp(jnp.where(incl, g_col - g_row, -jnp.inf))
    l_mat = jnp.where(strict, kk * decay, 0.0) * b_row
    t_mat = _pair_tri_inverse_minus_eye(l_mat, r, c) + eye
    eg_row = jnp.exp(g_row)
    zero_c = jnp.zeros((npair, CHUNK, HEAD_DIM), BF16)
    u_all = _bdot(t_mat.astype(BF16), _lane_diag(v3, zero_c))
    w_all = _bdot((t_mat * eg_row).astype(BF16),
                  _lane_diag(jnp.concatenate([k3, k3], axis=2), zero_c))
    lhs_o = jnp.concatenate([qk * decay * b_row, eye * eg_row], axis=2).astype(BF16)

    g_last = jnp.where(low[:, :1], jnp.broadcast_to(g_row[:, :, last:last + 1], g_row.shape),
                       jnp.broadcast_to(g_row[:, :, CHUNK + last:CHUNK + last + 1], g_row.shape))
    kscale = jnp.exp(g_last - g_row) * b_row
    e_last = jnp.exp(g_last)
    e_last = jnp.concatenate([jnp.broadcast_to(e_last[:, :, :1], (npair, 1, HEAD_DIM)),
                              jnp.broadcast_to(e_last[:, :, CHUNK:CHUNK + 1], (npair, 1, HEAD_DIM))],
                             axis=2)

    def kt_pair(b, ci, j):
        x = kt_ref[b, head(j), (ci // 2) * 2 * CHUNK:(ci // 2 + 1) * 2 * CHUNK]
        rolled = pltpu.roll(x, CHUNK, 1)
        return jnp.where(low[0, :1], x, rolled) if ci % 2 == 0 else jnp.where(low[0, :1], rolled, x)

    kgt = (jnp.stack([kt_pair(b, ci, j) for b, ci, j in index]) * kscale).astype(BF16)

    def of_chunk(x, ci):
        x = x.reshape((nb, nc, kh_n) + x.shape[1:])
        return x[:, ci].reshape((nb * kh_n,) + x.shape[3:])

    state = s_ref[...]
    groups = nb * kh_n
    zero_s = jnp.zeros((groups, HEAD_DIM, HEAD_DIM), BF16)
    zero_g = jnp.zeros((groups, CHUNK, HEAD_DIM), BF16)
    order = range(nc - 1, -1, -1) if reverse else range(nc)
    for ci in order:
        q_c = of_chunk(q3, ci)
        lhs = jnp.concatenate([of_chunk(w_all, ci).astype(BF16),
                               jnp.concatenate([q_c, q_c], axis=2)], axis=1)
        ws_qs = _bdot(lhs, _lane_diag(state.astype(BF16), zero_s))
        v_new = (of_chunk(u_all, ci) - ws_qs[:, :CHUNK]).astype(BF16)
        vn_diag = _lane_diag(v_new, zero_g)
        qs_diag = _lane_diag(ws_qs[:, CHUNK:].astype(BF16), zero_g)
        o = _bdot(of_chunk(lhs_o, ci), jnp.concatenate([vn_diag, qs_diag], axis=1))
        state = state * of_chunk(e_last, ci) + _bdot(of_chunk(kgt, ci), vn_diag)
        for b in range(nb):
            for j in range(kh_n):
                o_bj = o[b * kh_n + j]
                cols = head(j, 2 * HEAD_DIM)
                if final:
                    tot = o_bj + of_ref[b, rows(ci), cols]
                    y = jnp.concatenate([_rms(tot[:, :HEAD_DIM], nw_ref[...]),
                                         _rms(tot[:, HEAD_DIM:], nw_ref[...])], axis=1)
                    o_ref[b, rows(ci), cols] = (y * gz_ref[b, rows(ci), cols]).astype(BF16)
                else:
                    o_ref[b, rows(ci), cols] = o_bj
    s_ref[...] = state


def _gdn_scan(q, k, kt, v, gc, gr, *, reverse, o_fwd=None, gz=None, norm_w=None):
    batch, seq, _ = q.shape
    final = o_fwd is not None
    tb = min(TB_SCAN, seq)
    nblk = seq // tb
    nc = tb // CHUNK
    blk = (lambda i: nblk - 1 - i) if reverse else (lambda i: i)
    tok = lambda width: pl.BlockSpec((batch, tb, width), lambda i: (0, blk(i), 0))
    in_specs = [tok(GDN_QK_DIM), tok(GDN_QK_DIM),
                pl.BlockSpec((batch, GDN_QK_DIM, tb), lambda i: (0, 0, blk(i))),
                tok(GDN_V_DIM), tok(GATE_LANES),
                pl.BlockSpec((batch, GDN_K_HEADS, nc, 4, 2 * CHUNK), lambda i: (0, 0, blk(i), 0, 0))]
    args = [q, k, kt, v, gc, gr]
    if final:
        in_specs += [tok(GDN_V_DIM), tok(GDN_V_DIM), _resident((1, HEAD_DIM))]
        args += [o_fwd, gz, norm_w]
    return pl.pallas_call(
        functools.partial(_gdn_scan_kernel, tb=tb, nb=batch, reverse=reverse, final=final),
        out_shape=jax.ShapeDtypeStruct((batch, seq, GDN_V_DIM), BF16 if final else F32),
        grid=(nblk,),
        in_specs=in_specs,
        out_specs=tok(GDN_V_DIM),
        scratch_shapes=[pltpu.VMEM((batch * GDN_K_HEADS, HEAD_DIM, 2 * HEAD_DIM), F32)],
        compiler_params=_params(1),
        name="gdn_scan_bwd" if reverse else "gdn_scan_fwd",
    )(*args)


def _rope_tables(seq):
    half = HEAD_DIM // 2
    inv_freq = 1.0 / (ROPE_THETA ** (jnp.arange(half, dtype=F32) / half))
    ang = jnp.arange(seq, dtype=F32)[:, None] * inv_freq[None, :]
    cos = jnp.cos(ang)
    sin = jnp.sin(ang)
    return jnp.concatenate([cos, cos], axis=-1), jnp.concatenate([-sin, sin], axis=-1)


def _gate_lane_order():
    kh = jnp.arange(GDN_K_HEADS)[:, None, None, None]
    a = jnp.arange(2)[None, :, None, None]
    d = jnp.arange(2)[None, None, :, None]
    kind = jnp.arange(2)[None, None, None, :]
    return (d * 2 * GDN_V_HEADS + kind * GDN_V_HEADS + 2 * kh + a).reshape(-1)


def _gate_vector(p):
    t = p.astype(F32).reshape(2, GDN_K_HEADS, 2).transpose(1, 2, 0)
    t = jnp.stack([jnp.zeros_like(t), t], axis=3)
    return jnp.pad(t.reshape(-1), (0, GATE_LANES - 4 * GDN_V_HEADS))


def _trunk(x, seq, norm_gains, attn_w_in, attn_sinks, attn_w_out, gdn_w_in, gdn_conv_w, gdn_A_log,
           gdn_dt_bias, gdn_norm_w, gdn_w_out, mlp_w_up, mlp_w_down):
    depth = norm_gains.shape[0]
    n = x.shape[0]
    batch = n // seq
    cos, sin = _rope_tables(seq)
    lane_order = _gate_lane_order()
    for i in range(depth):
        j = i // 2
        gains = norm_gains[i].astype(F32)
        if i % 2 == 0:
            q, k, v = _attn_in(x, gains, attn_w_in[j].astype(BF16), cos, sin, seq)
            mixed = _attn(q, k, v, attn_sinks[j].astype(F32), seq)
            w_out = attn_w_out[j]
        else:
            w = gdn_w_in[j]
            w_gate = w[:, GDN_CONV_DIM + GDN_V_DIM:][:, lane_order]
            w_gate = jnp.pad(w_gate, ((0, 0), (0, GATE_LANES - w_gate.shape[1])))
            gate_params = jnp.stack([_gate_vector(gdn_A_log[j]), _gate_vector(gdn_dt_bias[j])])
            q, k, kt, v, gz, gc, gr = _gdn_in(
                x, gains, w[:, :GDN_CONV_DIM].astype(BF16),
                w[:, GDN_CONV_DIM:GDN_CONV_DIM + GDN_V_DIM].astype(BF16), w_gate.astype(BF16),
                gdn_conv_w[j].astype(F32), gate_params, seq)
            by_seq = lambda t: t.reshape(batch, seq, t.shape[-1])
            q, k, v, gz, gc = by_seq(q), by_seq(k), by_seq(v), by_seq(gz), by_seq(gc)
            o_fwd = _gdn_scan(q, k, kt, v, gc, gr, reverse=False)
            mixed = _gdn_scan(q, k, kt, v, gc, gr, reverse=True, o_fwd=o_fwd, gz=gz,
                              norm_w=gdn_norm_w[j].astype(F32).reshape(1, HEAD_DIM))
            mixed = mixed.reshape(n, GDN_V_DIM)
            w_out = gdn_w_out[j]
        x = _out_mlp(mixed, x, w_out.astype(BF16), gains, mlp_w_up[i].astype(BF16),
                     mlp_w_down[i].astype(BF16))
    return x


def kernel(x_prompt, x_sample, norm_gains, attn_w_in, attn_sinks, attn_w_out, gdn_w_in, gdn_conv_w,
           gdn_A_log, gdn_dt_bias, gdn_norm_w, gdn_w_out, mlp_w_up, mlp_w_down):
    seq = x_prompt.shape[1]
    assert x_sample.shape[1] == seq and x_prompt.shape[2] == D_MODEL
    x = jnp.concatenate([x_prompt.reshape(-1, D_MODEL), x_sample.reshape(-1, D_MODEL)], axis=0)
    y = _trunk(x, seq, norm_gains, attn_w_in, attn_sinks, attn_w_out, gdn_w_in, gdn_conv_w,
               gdn_A_log, gdn_dt_bias, gdn_norm_w, gdn_w_out, mlp_w_up, mlp_w_down)
    n_prompt = x_prompt.shape[0] * seq
    return (y[:n_prompt].reshape(x_prompt.shape), y[n_prompt:].reshape(x_sample.shape))
```

```python
import functools

import jax
import jax.numpy as jnp
from jax import lax
from jax.experimental import pallas as pl
from jax.experimental.pallas import tpu as pltpu

F32 = jnp.float32
BF16 = jnp.bfloat16

D_MODEL = 1024
D_FF = 4 * D_MODEL
RMS_EPS = 1e-6
HEAD_DIM = 128

ATTN_Q_HEADS = 8
ATTN_KV_HEADS = 2
ATTN_GROUP = ATTN_Q_HEADS // ATTN_KV_HEADS
ATTN_Q_DIM = ATTN_Q_HEADS * HEAD_DIM
ATTN_KV_DIM = ATTN_KV_HEADS * HEAD_DIM
ATTN_BLOCK = 128
ROPE_THETA = 10000.0

GDN_K_HEADS = 8
GDN_V_HEADS = 16
GDN_QK_DIM = GDN_K_HEADS * HEAD_DIM
GDN_V_DIM = GDN_V_HEADS * HEAD_DIM
GDN_CONV_DIM = 2 * GDN_QK_DIM + GDN_V_DIM
CONV_WIDTH = 5
CONV_PAD = CONV_WIDTH // 2
CHUNK = 64
GATE_LANES = 128
HALO = 16

VMEM_LIMIT_BYTES = 56 * 1024 * 1024

TM_PROJ = 512
TQ_ATTN = 512
TB_SCAN = 128
FF_CHUNK = 512
CONV_COLS = 512


def _rms(x, w):
    return x * lax.rsqrt(jnp.mean(x * x, axis=-1, keepdims=True) + RMS_EPS) * w


def _silu(x):
    return x * jax.nn.sigmoid(x)


def _dot(a, b):
    return jnp.dot(a, b, preferred_element_type=F32)


def _bdot(a, b):
    return lax.dot_general(a, b, (((2,), (1,)), ((0,), (0,))), preferred_element_type=F32)


def _bdot_nt(a, b):
    return lax.dot_general(a, b, (((2,), (2,)), ((0,), (0,))), preferred_element_type=F32)


def _resident(shape):
    zeros = (0,) * len(shape)
    return pl.BlockSpec(shape, lambda *_: zeros, pipeline_mode=pl.Buffered(1))


def _params(n_axes):
    return pltpu.CompilerParams(dimension_semantics=("arbitrary",) * n_axes,
                                vmem_limit_bytes=VMEM_LIMIT_BYTES)


def _out_mlp_kernel(a_ref, x_ref, wo_ref, g_ref, wu_ref, wd_ref, o_ref):
    h = _dot(a_ref[...], wo_ref[...])
    x1 = x_ref[...] + _rms(h, g_ref[1:2, :])
    hn = _rms(x1, g_ref[2:3, :]).astype(BF16)
    acc = jnp.zeros(x1.shape, F32)
    for c in range(D_FF // FF_CHUNK):
        cols = slice(c * FF_CHUNK, (c + 1) * FF_CHUNK)
        up = _dot(hn, wu_ref[:, cols])
        act = jnp.square(jnp.maximum(up, 0.0)).astype(BF16)
        acc = acc + _dot(act, wd_ref[cols, :])
    o_ref[...] = x1 + _rms(acc, g_ref[3:4, :])


def _out_mlp(a, x, w_out, gains, w_up, w_down):
    n, k = a.shape
    tm = min(TM_PROJ, n)
    return pl.pallas_call(
        _out_mlp_kernel,
        out_shape=jax.ShapeDtypeStruct((n, D_MODEL), F32),
        grid=(n // tm,),
        in_specs=[
            pl.BlockSpec((tm, k), lambda i: (i, 0)),
            pl.BlockSpec((tm, D_MODEL), lambda i: (i, 0)),
            _resident((k, D_MODEL)),
            _resident((4, D_MODEL)),
            _resident((D_MODEL, D_FF)),
            _resident((D_FF, D_MODEL)),
        ],
        out_specs=pl.BlockSpec((tm, D_MODEL), lambda i: (i, 0)),
        compiler_params=_params(1),
        name="out_mlp",
    )(a, x, w_out, gains, w_up, w_down)


def _attn_in_kernel(x_ref, g_ref, w_ref, cos_ref, sin_ref, q_ref, k_ref, v_ref):
    hn = _rms(x_ref[...], g_ref[0:1, :]).astype(BF16)
    p = _dot(hn, w_ref[...])
    cos = cos_ref[...]
    sin = sin_ref[...]

    def rope(t):
        return t * cos + pltpu.roll(t, HEAD_DIM // 2, 1) * sin

    for h in range(ATTN_Q_HEADS):
        cols = slice(h * HEAD_DIM, (h + 1) * HEAD_DIM)
        q_ref[:, cols] = rope(p[:, cols]).astype(BF16)
    for h in range(ATTN_KV_HEADS):
        cols = slice(h * HEAD_DIM, (h + 1) * HEAD_DIM)
        src = slice(ATTN_Q_DIM + h * HEAD_DIM, ATTN_Q_DIM + (h + 1) * HEAD_DIM)
        k_ref[:, cols] = rope(p[:, src]).astype(BF16)
    v_ref[...] = p[:, ATTN_Q_DIM + ATTN_KV_DIM:].astype(BF16)


def _attn_in(x, gains, w_in, cos, sin, seq):
    n = x.shape[0]
    tm = min(TM_PROJ, seq)
    per_seq = seq // tm
    return pl.pallas_call(
        _attn_in_kernel,
        out_shape=(jax.ShapeDtypeStruct((n, ATTN_Q_DIM), BF16),
                   jax.ShapeDtypeStruct((n, ATTN_KV_DIM), BF16),
                   jax.ShapeDtypeStruct((n, ATTN_KV_DIM), BF16)),
        grid=(n // tm,),
        in_specs=[
            pl.BlockSpec((tm, D_MODEL), lambda i: (i, 0)),
            _resident((4, D_MODEL)),
            _resident(w_in.shape),
            pl.BlockSpec((tm, HEAD_DIM), lambda i: (i % per_seq, 0)),
            pl.BlockSpec((tm, HEAD_DIM), lambda i: (i % per_seq, 0)),
        ],
        out_specs=(pl.BlockSpec((tm, ATTN_Q_DIM), lambda i: (i, 0)),
                   pl.BlockSpec((tm, ATTN_KV_DIM), lambda i: (i, 0)),
                   pl.BlockSpec((tm, ATTN_KV_DIM), lambda i: (i, 0))),
        compiler_params=_params(1),
        name="attn_in",
    )(x, gains, w_in, cos, sin)


def _attn_kernel(sink_ref, q_ref, kp_ref, km_ref, kn_ref, vp_ref, vm_ref, vn_ref, o_ref, *, tq, seq):
    t = pl.program_id(0)
    p0 = (t * tq) % seq
    prev_bad = p0 == 0
    next_bad = (p0 + tq) == seq
    kfull = jnp.concatenate([kp_ref[...], km_ref[...], kn_ref[...]], axis=0)
    vfull = jnp.concatenate([vp_ref[...], vm_ref[...], vn_ref[...]], axis=0)

    rows = ATTN_GROUP * ATTN_BLOCK
    cols = 3 * ATTN_BLOCK
    r = lax.broadcasted_iota(jnp.int32, (rows, cols), 0) % ATTN_BLOCK
    cc = lax.broadcasted_iota(jnp.int32, (rows, cols), 1)
    cb = cc // ATTN_BLOCK
    c = cc % ATTN_BLOCK
    band_ok = ((cb == 0) & (c >= r)) | (cb == 1) | ((cb == 2) & (c <= r))
    nblk = tq // ATTN_BLOCK
    index = [(jb, h) for jb in range(nblk) for h in range(ATTN_KV_HEADS)]

    def head_cols(h, g):
        return slice((h * ATTN_GROUP + g) * HEAD_DIM, (h * ATTN_GROUP + g + 1) * HEAD_DIM)

    def key_rows(jb):
        return slice(jb * ATTN_BLOCK, jb * ATTN_BLOCK + cols)

    q3 = jnp.stack([jnp.concatenate([q_ref[jb * ATTN_BLOCK:(jb + 1) * ATTN_BLOCK, head_cols(h, g)]
                                     for g in range(ATTN_GROUP)], axis=0) for jb, h in index])
    k3 = jnp.stack([kfull[key_rows(jb), h * HEAD_DIM:(h + 1) * HEAD_DIM] for jb, h in index])
    v3 = jnp.stack([vfull[key_rows(jb), h * HEAD_DIM:(h + 1) * HEAD_DIM] for jb, h in index])
    s = _bdot_nt(q3, k3) * (HEAD_DIM ** -0.5)

    def valid(jb):
        ok = band_ok
        if jb == 0:
            ok = ok & (cc >= jnp.where(prev_bad, ATTN_BLOCK, 0))
        if jb == nblk - 1:
            ok = ok & (cc < jnp.where(next_bad, 2 * ATTN_BLOCK, cols))
        return ok

    s = jnp.stack([jnp.where(valid(jb), s[i], -jnp.inf) for i, (jb, h) in enumerate(index)])
    sink = jnp.stack([jnp.concatenate(
        [jnp.full((ATTN_BLOCK, 1), sink_ref[h * ATTN_GROUP + g], F32) for g in range(ATTN_GROUP)], axis=0)
        for jb, h in index])
    m = jnp.maximum(jnp.max(s, axis=-1, keepdims=True), sink)
    p = jnp.exp(s - m)
    denom = jnp.sum(p, axis=-1, keepdims=True) + jnp.exp(sink - m)
    o = _bdot((p / denom).astype(BF16), v3)
    for i, (jb, h) in enumerate(index):
        for g in range(ATTN_GROUP):
            o_ref[jb * ATTN_BLOCK:(jb + 1) * ATTN_BLOCK, head_cols(h, g)] = (
                o[i, g * ATTN_BLOCK:(g + 1) * ATTN_BLOCK, :].astype(BF16))


def _attn(q, k, v, sinks, seq):
    n = q.shape[0]
    tq = min(TQ_ATTN, seq)
    per = tq // ATTN_BLOCK
    last = n // ATTN_BLOCK - 1
    prev_map = lambda t: (jnp.maximum(t * per - 1, 0), 0)
    next_map = lambda t: (jnp.minimum((t + 1) * per, last), 0)
    kv_main = pl.BlockSpec((tq, ATTN_KV_DIM), lambda t: (t, 0))
    kv_prev = pl.BlockSpec((ATTN_BLOCK, ATTN_KV_DIM), prev_map)
    kv_next = pl.BlockSpec((ATTN_BLOCK, ATTN_KV_DIM), next_map)
    return pl.pallas_call(
        functools.partial(_attn_kernel, tq=tq, seq=seq),
        out_shape=jax.ShapeDtypeStruct((n, ATTN_Q_DIM), BF16),
        grid=(n // tq,),
        in_specs=[
            pl.BlockSpec(memory_space=pltpu.SMEM),
            pl.BlockSpec((tq, ATTN_Q_DIM), lambda t: (t, 0)),
            kv_prev, kv_main, kv_next, kv_prev, kv_main, kv_next,
        ],
        out_specs=pl.BlockSpec((tq, ATTN_Q_DIM), lambda t: (t, 0)),
        compiler_params=_params(1),
        name="attn_core",
    )(sinks, q, k, k, k, v, v, v)


def _gdn_in_kernel(xp_ref, xm_ref, xn_ref, g_ref, wc_ref, wz_ref, wg_ref, cw_ref, gp_ref,
                   q_ref, k_ref, kt_ref, v_ref, gz_ref, gc_ref, gr_ref, *, tm, seq):
    i = pl.program_id(0)
    p0 = (i * tm) % seq
    gain = g_ref[0:1, :]
    xp = jnp.where(p0 == 0, 0.0, xp_ref[...])
    xn = jnp.where(p0 + tm == seq, 0.0, xn_ref[...])
    hn = _rms(jnp.concatenate([xp, xm_ref[...], xn], axis=0), gain).astype(BF16)
    hm = hn[HALO:HALO + tm, :]

    qk_scale = HEAD_DIM ** -0.5
    for c in range(GDN_CONV_DIM // CONV_COLS):
        cols = slice(c * CONV_COLS, (c + 1) * CONV_COLS)
        pc = _dot(hn, wc_ref[:, cols])
        acc = None
        for j in range(CONV_WIDTH):
            lo = HALO - CONV_PAD + j
            term = pc[lo:lo + tm, :] * cw_ref[j:j + 1, cols]
            acc = term if acc is None else acc + term
        act = _silu(acc)
        for hh in range(CONV_COLS // HEAD_DIM):
            col0 = c * CONV_COLS + hh * HEAD_DIM
            t = act[:, hh * HEAD_DIM:(hh + 1) * HEAD_DIM]
            if col0 < 2 * GDN_QK_DIM:
                t = t * lax.rsqrt(jnp.sum(t * t, axis=-1, keepdims=True) + RMS_EPS)
            if col0 < GDN_QK_DIM:
                q_ref[:, col0:col0 + HEAD_DIM] = (t * qk_scale).astype(BF16)
            elif col0 < 2 * GDN_QK_DIM:
                kc0 = col0 - GDN_QK_DIM
                k_ref[:, kc0:kc0 + HEAD_DIM] = t.astype(BF16)
                kt_ref[0, kc0:kc0 + HEAD_DIM, :] = t.T
            else:
                vc0 = col0 - 2 * GDN_QK_DIM
                v_ref[:, vc0:vc0 + HEAD_DIM] = t.astype(BF16)

    for c in range(GDN_V_DIM // CONV_COLS):
        cols = slice(c * CONV_COLS, (c + 1) * CONV_COLS)
        gz_ref[:, cols] = _silu(_dot(hm, wz_ref[:, cols]))

    ga = _dot(hm, wg_ref[...])
    lane = lax.broadcasted_iota(jnp.int32, (1, GATE_LANES), 1)
    is_decay = (lane & 1) != 0
    is_bwd = (lane & 2) != 0
    beta = jax.nn.sigmoid(ga)
    pre = ga + gp_ref[1:2, :]
    softplus = jnp.maximum(pre, 0.0) + jnp.log1p(jnp.exp(-jnp.abs(pre)))
    gval = jnp.where(is_decay, -jnp.exp(gp_ref[0:1, :]) * softplus, 0.0)
    rr = lax.broadcasted_iota(jnp.int32, (tm, tm), 0)
    cc = lax.broadcasted_iota(jnp.int32, (tm, tm), 1)
    same = (rr // CHUNK) == (cc // CHUNK)
    tri_f = (same & (rr >= cc)).astype(F32)
    tri_b = (same & (rr <= cc)).astype(F32)
    cum_f = jnp.dot(tri_f, gval, preferred_element_type=F32, precision=lax.Precision.HIGHEST)
    cum_b = jnp.dot(tri_b, gval, preferred_element_type=F32, precision=lax.Precision.HIGHEST)
    gates = jnp.where(is_decay, jnp.where(is_bwd, cum_b, cum_f), beta)
    gc_ref[...] = gates
    gates_t = gates.T
    low_half = lax.broadcasted_iota(jnp.int32, (4, 2 * CHUNK), 1) < CHUNK
    for kh in range(GDN_K_HEADS):
        for m in range(tm // (2 * CHUNK)):
            tok = slice(m * 2 * CHUNK, (m + 1) * 2 * CHUNK)
            head_a = gates_t[8 * kh:8 * kh + 4, tok]
            head_b = gates_t[8 * kh + 4:8 * kh + 8, tok]
            gr_ref[0, kh, 2 * m] = jnp.where(low_half, head_a, pltpu.roll(head_b, CHUNK, 1))
            gr_ref[0, kh, 2 * m + 1] = jnp.where(low_half, pltpu.roll(head_a, CHUNK, 1), head_b)


def _gdn_in(x, gains, w_conv, w_z, w_gate, conv_w, gate_params, seq):
    n = x.shape[0]
    batch = n // seq
    tm = min(TM_PROJ, seq)
    per_seq = seq // tm
    per = tm // HALO
    last = n // HALO - 1
    row = lambda i: (i, 0)
    return pl.pallas_call(
        functools.partial(_gdn_in_kernel, tm=tm, seq=seq),
        out_shape=(jax.ShapeDtypeStruct((n, GDN_QK_DIM), BF16),
                   jax.ShapeDtypeStruct((n, GDN_QK_DIM), BF16),
                   jax.ShapeDtypeStruct((batch, GDN_QK_DIM, seq), F32),
                   jax.ShapeDtypeStruct((n, GDN_V_DIM), BF16),
                   jax.ShapeDtypeStruct((n, GDN_V_DIM), F32),
                   jax.ShapeDtypeStruct((n, GATE_LANES), F32),
                   jax.ShapeDtypeStruct((batch, GDN_K_HEADS, seq // CHUNK, 4, 2 * CHUNK), F32)),
        grid=(n // tm,),
        in_specs=[
            pl.BlockSpec((HALO, D_MODEL), lambda i: (jnp.maximum(i * per - 1, 0), 0)),
            pl.BlockSpec((tm, D_MODEL), row),
            pl.BlockSpec((HALO, D_MODEL), lambda i: (jnp.minimum((i + 1) * per, last), 0)),
            _resident((4, D_MODEL)),
            _resident(w_conv.shape),
            _resident(w_z.shape),
            _resident(w_gate.shape),
            _resident(conv_w.shape),
            _resident(gate_params.shape),
        ],
        out_specs=(pl.BlockSpec((tm, GDN_QK_DIM), row),
                   pl.BlockSpec((tm, GDN_QK_DIM), row),
                   pl.BlockSpec((1, GDN_QK_DIM, tm), lambda i: (i // per_seq, 0, i % per_seq)),
                   pl.BlockSpec((tm, GDN_V_DIM), row),
                   pl.BlockSpec((tm, GDN_V_DIM), row),
                   pl.BlockSpec((tm, GATE_LANES), row),
                   pl.BlockSpec((1, GDN_K_HEADS, tm // CHUNK, 4, 2 * CHUNK),
                                lambda i: (i // per_seq, 0, i % per_seq, 0, 0))),
        compiler_params=_params(1),
        name="gdn_in",
    )(x, x, x, gains, w_conv, w_z, w_gate, conv_w, gate_params)


def _pair_diag(x):
    half = x.shape[2] // 2
    low = lax.broadcasted_iota(jnp.int32, x.shape, 2) < half
    return jnp.concatenate([jnp.where(low, x, 0.0), jnp.where(low, 0.0, x)], axis=1)


def _lane_diag(x, zero):
    half = x.shape[2] // 2
    return jnp.concatenate([jnp.concatenate([x[:, :, :half], zero], axis=2),
                            jnp.concatenate([zero, x[:, :, half:]], axis=2)], axis=1)


def _pair_tri_inverse_minus_eye(l_mat, r, c):
    e = -jnp.where((r // 2) == (c // 2), l_mat, 0.0)
    b = 2
    while b < CHUNK:
        cb = jnp.where(((r // (2 * b)) == (c // (2 * b))) & ((r // b) != (c // b)), l_mat, 0.0)
        z = cb + _bdot(cb.astype(BF16), _pair_diag(e).astype(BF16))
        y = z + _bdot(e.astype(BF16), _pair_diag(z).astype(BF16))
        e = e - y
        b *= 2
    return e


def _gdn_scan_kernel(*refs, tb, nb, reverse, final):
    if final:
        (q_ref, k_ref, kt_ref, v_ref, gc_ref, gr_ref, of_ref, gz_ref, nw_ref, o_ref, s_ref) = refs
    else:
        (q_ref, k_ref, kt_ref, v_ref, gc_ref, gr_ref, o_ref, s_ref) = refs

    @pl.when(pl.program_id(0) == 0)
    def _():
        s_ref[...] = jnp.zeros(s_ref.shape, F32)

    nc = tb // CHUNK
    kh_n = GDN_K_HEADS
    index = [(b, ci, j) for b in range(nb) for ci in range(nc) for j in range(kh_n)]
    npair = len(index)
    d = 1 if reverse else 0
    last = 0 if reverse else CHUNK - 1

    def rows(ci):
        return slice(ci * CHUNK, (ci + 1) * CHUNK)

    def head(j, width=HEAD_DIM):
        return slice(j * width, (j + 1) * width)

    r = lax.broadcasted_iota(jnp.int32, (1, CHUNK, 2 * CHUNK), 1)
    lane = lax.broadcasted_iota(jnp.int32, (1, CHUNK, 2 * CHUNK), 2)
    c = lane % CHUNK
    low = lane < CHUNK
    incl = (r <= c) if reverse else (r >= c)
    strict = (r < c) if reverse else (r > c)
    eye = (r == c).astype(F32)

    q3 = jnp.stack([q_ref[b, rows(ci), head(j)] for b, ci, j in index])
    k3 = jnp.stack([k_ref[b, rows(ci), head(j)] for b, ci, j in index])
    v3 = jnp.stack([v_ref[b, rows(ci), head(j, 2 * HEAD_DIM)] for b, ci, j in index])
    prod = _bdot_nt(jnp.concatenate([q3, k3], axis=1), jnp.concatenate([k3, k3], axis=1))
    qk = prod[:, :CHUNK]
    kk = prod[:, CHUNK:]

    def g_column(b, ci, j):
        la = 8 * j + 2 * d + 1
        col_a = jnp.broadcast_to(gc_ref[b, rows(ci), la:la + 1], (CHUNK, 2 * CHUNK))
        col_b = jnp.broadcast_to(gc_ref[b, rows(ci), la + 4:la + 5], (CHUNK, 2 * CHUNK))
        return jnp.where(low[0], col_a, col_b)

    g_col = jnp.stack([g_column(b, ci, j) for b, ci, j in index])
    g_row = jnp.stack([gr_ref[b, j, ci, 2 * d + 1:2 * d + 2, :] for b, ci, j in index])
    b_row = jnp.stack([gr_ref[b, j, ci, 2 * d:2 * d + 1, :] for b, ci, j in index])

    decay = jnp.exp(jnp.where(incl, g_col - g_row, -jnp.inf))
    l_mat = jnp.where(strict, kk * decay, 0.0) * b_row
    t_mat = _pair_tri_inverse_minus_eye(l_mat, r, c) + eye
    eg_row = jnp.exp(g_row)
    zero_c = jnp.zeros((npair, CHUNK, HEAD_DIM), BF16)
    u_all = _bdot(t_mat.astype(BF16), _lane_diag(v3, zero_c))
    w_all = _bdot((t_mat * eg_row).astype(BF16),
                  _lane_diag(jnp.concatenate([k3, k3], axis=2), zero_c))
    aq = qk * decay * b_row
    dg = eye * pltpu.roll(eg_row.reshape(npair, 2 * CHUNK), CHUNK, 1)[:, None, :]
    lhs_o_a = jnp.where(low, aq, dg).astype(BF16)
    lhs_o_b = jnp.where(low, dg, aq).astype(BF16)

    g_last = jnp.where(low[:, :1], jnp.broadcast_to(g_row[:, :, last:last + 1], g_row.shape),
                       jnp.broadcast_to(g_row[:, :, CHUNK + last:CHUNK + last + 1], g_row.shape))
    kscale = jnp.exp(g_last - g_row) * b_row
    e_last = jnp.exp(g_last)
    e_last = jnp.concatenate([jnp.broadcast_to(e_last[:, :, :1], (npair, 1, HEAD_DIM)),
                              jnp.broadcast_to(e_last[:, :, CHUNK:CHUNK + 1], (npair, 1, HEAD_DIM))],
                             axis=2)

    ks2 = kscale.reshape(npair, 2 * CHUNK)
    ks2_rolled = pltpu.roll(ks2, CHUNK, 1)
    low2 = low[0, :1]
    zero2 = jnp.zeros_like(ks2)
    odd = jnp.stack([jnp.full((1, 2 * CHUNK), ci % 2 == 1) for b, ci, j in index]).reshape(npair, 2 * CHUNK)
    srow_a = jnp.where(odd, jnp.where(low2, zero2, ks2_rolled), jnp.where(low2, ks2, zero2))
    srow_b = jnp.where(odd, jnp.where(low2, zero2, ks2), jnp.where(low2, ks2_rolled, zero2))
    kt_tile = jnp.stack([kt_ref[b, head(j), (ci // 2) * 2 * CHUNK:(ci // 2 + 1) * 2 * CHUNK]
                         for b, ci, j in index])
    kgt_a = (kt_tile * srow_a[:, None, :]).astype(BF16)
    kgt_b = (kt_tile * srow_b[:, None, :]).astype(BF16)

    def of_chunk(x, ci):
        x = x.reshape((nb, nc, kh_n) + x.shape[1:])
        return x[:, ci].reshape((nb * kh_n,) + x.shape[3:])

    state = s_ref[...]
    groups = nb * kh_n
    zero_g = jnp.zeros((groups, CHUNK, HEAD_DIM), BF16)
    order = range(nc - 1, -1, -1) if reverse else range(nc)
    for ci in order:
        q_c = of_chunk(q3, ci)
        w_c = of_chunk(w_all, ci).astype(BF16)
        u_c = of_chunk(u_all, ci)
        s16 = state.astype(BF16)
        r_a = _bdot(jnp.concatenate([w_c[:, :, :HEAD_DIM], q_c], axis=1), s16[:, :, :HEAD_DIM])
        r_b = _bdot(jnp.concatenate([w_c[:, :, HEAD_DIM:], q_c], axis=1), s16[:, :, HEAD_DIM:])
        vn_a = (u_c[:, :, :HEAD_DIM] - r_a[:, :CHUNK]).astype(BF16)
        vn_b = (u_c[:, :, HEAD_DIM:] - r_b[:, :CHUNK]).astype(BF16)
        o_a = _bdot(of_chunk(lhs_o_a, ci), jnp.concatenate([vn_a, r_a[:, CHUNK:].astype(BF16)], axis=1))
        o_b = _bdot(of_chunk(lhs_o_b, ci), jnp.concatenate([r_b[:, CHUNK:].astype(BF16), vn_b], axis=1))
        o = jnp.concatenate([o_a, o_b], axis=2)
        pad_a = [vn_a, zero_g] if ci % 2 == 0 else [zero_g, vn_a]
        pad_b = [vn_b, zero_g] if ci % 2 == 0 else [zero_g, vn_b]
        upd = jnp.concatenate([_bdot(of_chunk(kgt_a, ci), jnp.concatenate(pad_a, axis=1)),
                               _bdot(of_chunk(kgt_b, ci), jnp.concatenate(pad_b, axis=1))], axis=2)
        state = state * of_chunk(e_last, ci) + upd
        for b in range(nb):
            for j in range(kh_n):
                o_bj = o[b * kh_n + j]
                cols = head(j, 2 * HEAD_DIM)
                if final:
                    tot = o_bj + of_ref[b, rows(ci), cols]
                    y = jnp.concatenate([_rms(tot[:, :HEAD_DIM], nw_ref[...]),
                                         _rms(tot[:, HEAD_DIM:], nw_ref[...])], axis=1)
                    o_ref[b, rows(ci), cols] = (y * gz_ref[b, rows(ci), cols]).astype(BF16)
                else:
                    o_ref[b, rows(ci), cols] = o_bj
    s_ref[...] = state


def _gdn_scan(q, k, kt, v, gc, gr, *, reverse, o_fwd=None, gz=None, norm_w=None):
    batch, seq, _ = q.shape
    final = o_fwd is not None
    tb = min(TB_SCAN, seq)
    nblk = seq // tb
    nc = tb // CHUNK
    blk = (lambda i: nblk - 1 - i) if reverse else (lambda i: i)
    tok = lambda width: pl.BlockSpec((batch, tb, width), lambda i: (0, blk(i), 0))
    in_specs = [tok(GDN_QK_DIM), tok(GDN_QK_DIM),
                pl.BlockSpec((batch, GDN_QK_DIM, tb), lambda i: (0, 0, blk(i))),
                tok(GDN_V_DIM), tok(GATE_LANES),
                pl.BlockSpec((batch, GDN_K_HEADS, nc, 4, 2 * CHUNK), lambda i: (0, 0, blk(i), 0, 0))]
    args = [q, k, kt, v, gc, gr]
    if final:
        in_specs += [tok(GDN_V_DIM), tok(GDN_V_DIM), _resident((1, HEAD_DIM))]
        args += [o_fwd, gz, norm_w]
    return pl.pallas_call(
        functools.partial(_gdn_scan_kernel, tb=tb, nb=batch, reverse=reverse, final=final),
        out_shape=jax.ShapeDtypeStruct((batch, seq, GDN_V_DIM), BF16 if final else F32),
        grid=(nblk,),
        in_specs=in_specs,
        out_specs=tok(GDN_V_DIM),
        scratch_shapes=[pltpu.VMEM((batch * GDN_K_HEADS, HEAD_DIM, 2 * HEAD_DIM), F32)],
        compiler_params=_params(1),
        name="gdn_scan_bwd" if reverse else "gdn_scan_fwd",
    )(*args)


def _rope_tables(seq):
    half = HEAD_DIM // 2
    inv_freq = 1.0 / (ROPE_THETA ** (jnp.arange(half, dtype=F32) / half))
    ang = jnp.arange(seq, dtype=F32)[:, None] * inv_freq[None, :]
    cos = jnp.cos(ang)
    sin = jnp.sin(ang)
    return jnp.concatenate([cos, cos], axis=-1), jnp.concatenate([-sin, sin], axis=-1)


def _gate_lane_order():
    kh = jnp.arange(GDN_K_HEADS)[:, None, None, None]
    a = jnp.arange(2)[None, :, None, None]
    d = jnp.arange(2)[None, None, :, None]
    kind = jnp.arange(2)[None, None, None, :]
    return (d * 2 * GDN_V_HEADS + kind * GDN_V_HEADS + 2 * kh + a).reshape(-1)


def _gate_vector(p):
    t = p.astype(F32).reshape(2, GDN_K_HEADS, 2).transpose(1, 2, 0)
    t = jnp.stack([jnp.zeros_like(t), t], axis=3)
    return jnp.pad(t.reshape(-1), (0, GATE_LANES - 4 * GDN_V_HEADS))


def _trunk(x, seq, norm_gains, attn_w_in, attn_sinks, attn_w_out, gdn_w_in, gdn_conv_w, gdn_A_log,
           gdn_dt_bias, gdn_norm_w, gdn_w_out, mlp_w_up, mlp_w_down):
    depth = norm_gains.shape[0]
    n = x.shape[0]
    batch = n // seq
    cos, sin = _rope_tables(seq)
    lane_order = _gate_lane_order()
    for i in range(depth):
        j = i // 2
        gains = norm_gains[i].astype(F32)
        if i % 2 == 0:
            q, k, v = _attn_in(x, gains, attn_w_in[j].astype(BF16), cos, sin, seq)
            mixed = _attn(q, k, v, attn_sinks[j].astype(F32), seq)
            w_out = attn_w_out[j]
        else:
            w = gdn_w_in[j]
            w_gate = w[:, GDN_CONV_DIM + GDN_V_DIM:][:, lane_order]
            w_gate = jnp.pad(w_gate, ((0, 0), (0, GATE_LANES - w_gate.shape[1])))
            gate_params = jnp.stack([_gate_vector(gdn_A_log[j]), _gate_vector(gdn_dt_bias[j])])
            q, k, kt, v, gz, gc, gr = _gdn_in(
                x, gains, w[:, :GDN_CONV_DIM].astype(BF16),
                w[:, GDN_CONV_DIM:GDN_CONV_DIM + GDN_V_DIM].astype(BF16), w_gate.astype(BF16),
                gdn_conv_w[j].astype(F32), gate_params, seq)
            by_seq = lambda t: t.reshape(batch, seq, t.shape[-1])
            q, k, v, gz, gc = by_seq(q), by_seq(k), by_seq(v), by_seq(gz), by_seq(gc)
            o_fwd = _gdn_scan(q, k, kt, v, gc, gr, reverse=False)
            mixed = _gdn_scan(q, k, kt, v, gc, gr, reverse=True, o_fwd=o_fwd, gz=gz,
                              norm_w=gdn_norm_w[j].astype(F32).reshape(1, HEAD_DIM))
            mixed = mixed.reshape(n, GDN_V_DIM)
            w_out = gdn_w_out[j]
        x = _out_mlp(mixed, x, w_out.astype(BF16), gains, mlp_w_up[i].astype(BF16),
                     mlp_w_down[i].astype(BF16))
    return x


def kernel(x_prompt, x_sample, norm_gains, attn_w_in, attn_sinks, attn_w_out, gdn_w_in, gdn_conv_w,
           gdn_A_log, gdn_dt_bias, gdn_norm_w, gdn_w_out, mlp_w_up, mlp_w_down):
    seq = x_prompt.shape[1]
    assert x_sample.shape[1] == seq and x_prompt.shape[2] == D_MODEL
    x = jnp.concatenate([x_prompt.reshape(-1, D_MODEL), x_sample.reshape(-1, D_MODEL)], axis=0)
    y = _trunk(x, seq, norm_gains, attn_w_in, attn_sinks, attn_w_out, gdn_w_in, gdn_conv_w,
               gdn_A_log, gdn_dt_bias, gdn_norm_w, gdn_w_out, mlp_w_up, mlp_w_down)
    n_prompt = x_prompt.shape[0] * seq
    return (y[:n_prompt].reshape(x_prompt.shape), y[n_prompt:].reshape(x_sample.shape))
```

```python
import functools

import jax
import jax.numpy as jnp
from jax import lax
from jax.experimental import pallas as pl
from jax.experimental.pallas import tpu as pltpu

F32 = jnp.float32
BF16 = jnp.bfloat16

D_MODEL = 1024
D_FF = 4 * D_MODEL
RMS_EPS = 1e-6
HEAD_DIM = 128

ATTN_Q_HEADS = 8
ATTN_KV_HEADS = 2
ATTN_GROUP = ATTN_Q_HEADS // ATTN_KV_HEADS
ATTN_Q_DIM = ATTN_Q_HEADS * HEAD_DIM
ATTN_KV_DIM = ATTN_KV_HEADS * HEAD_DIM
ATTN_BLOCK = 128
ROPE_THETA = 10000.0

GDN_K_HEADS = 8
GDN_V_HEADS = 16
GDN_QK_DIM = GDN_K_HEADS * HEAD_DIM
GDN_V_DIM = GDN_V_HEADS * HEAD_DIM
GDN_CONV_DIM = 2 * GDN_QK_DIM + GDN_V_DIM
CONV_WIDTH = 5
CONV_PAD = CONV_WIDTH // 2
CHUNK = 64
GATE_LANES = 128
HALO = 16

VMEM_LIMIT_BYTES = 56 * 1024 * 1024

TM_PROJ = 512
TQ_ATTN = 512
TB_SCAN = 128
ATTN_PART = 2
FF_CHUNK = 512
CONV_COLS = 512


def _rms(x, w):
    return x * lax.rsqrt(jnp.mean(x * x, axis=-1, keepdims=True) + RMS_EPS) * w


def _silu(x):
    return x * jax.nn.sigmoid(x)


def _dot(a, b):
    return jnp.dot(a, b, preferred_element_type=F32)


def _bdot(a, b):
    return lax.dot_general(a, b, (((2,), (1,)), ((0,), (0,))), preferred_element_type=F32)


def _bdot_nt(a, b):
    return lax.dot_general(a, b, (((2,), (2,)), ((0,), (0,))), preferred_element_type=F32)


def _resident(shape):
    zeros = (0,) * len(shape)
    return pl.BlockSpec(shape, lambda *_: zeros, pipeline_mode=pl.Buffered(1))


def _row_tile_specs(arrays, tm, first_tile=0):
    specs, starts, start = [], [], 0
    for arr in arrays:
        ntiles = arr.shape[0] // tm
        specs.append(pl.BlockSpec(
            (tm, arr.shape[1]),
            lambda i, s=start, n=ntiles: (jnp.clip(i + first_tile - s, 0, n - 1), 0)))
        starts.append(start - first_tile)
        start += ntiles
    return specs, tuple(starts)


def _pick_row_tile(refs, starts):
    i = pl.program_id(0)
    x = refs[0][...]
    for ref, start in zip(refs[1:], starts[1:]):
        x = jnp.where(i >= start, ref[...], x)
    return x


def _params(n_axes):
    return pltpu.CompilerParams(dimension_semantics=("arbitrary",) * n_axes,
                                vmem_limit_bytes=VMEM_LIMIT_BYTES)


def _out_mlp_kernel(*refs, starts):
    x_refs = refs[1:1 + len(starts)]
    a_ref = refs[0]
    wo_ref, g_ref, wu_ref, wd_ref, o_ref = refs[1 + len(starts):]
    h = _dot(a_ref[...], wo_ref[...])
    x1 = _pick_row_tile(x_refs, starts) + _rms(h, g_ref[1:2, :])
    hn = _rms(x1, g_ref[2:3, :]).astype(BF16)
    acc = jnp.zeros(x1.shape, F32)
    for c in range(D_FF // FF_CHUNK):
        cols = slice(c * FF_CHUNK, (c + 1) * FF_CHUNK)
        up = _dot(hn, wu_ref[:, cols])
        act = jnp.square(jnp.maximum(up, 0.0)).astype(BF16)
        acc = acc + _dot(act, wd_ref[cols, :])
    o_ref[...] = x1 + _rms(acc, g_ref[3:4, :])


def _out_mlp(a, xs, w_out, gains, w_up, w_down, row0=0, nrows=None):
    n, k = a.shape
    nrows = n - row0 if nrows is None else nrows
    tm = min(TM_PROJ, nrows)
    first = row0 // tm
    x_specs, starts = _row_tile_specs(xs, tm, first)
    return pl.pallas_call(
        functools.partial(_out_mlp_kernel, starts=starts),
        out_shape=jax.ShapeDtypeStruct((nrows, D_MODEL), F32),
        grid=(nrows // tm,),
        in_specs=[pl.BlockSpec((tm, k), lambda i: (i + first, 0))] + x_specs + [
            _resident((k, D_MODEL)),
            _resident((4, D_MODEL)),
            _resident((D_MODEL, D_FF)),
            _resident((D_FF, D_MODEL)),
        ],
        out_specs=pl.BlockSpec((tm, D_MODEL), lambda i: (i, 0)),
        compiler_params=_params(1),
        name="out_mlp",
    )(a, *xs, w_out, gains, w_up, w_down)


def _attn_in_kernel(*refs, starts):
    x_refs = refs[:len(starts)]
    g_ref, w_ref, cos_ref, sin_ref, q_ref, k_ref, v_ref = refs[len(starts):]
    hn = _rms(_pick_row_tile(x_refs, starts), g_ref[0:1, :]).astype(BF16)
    p = _dot(hn, w_ref[...])
    cos = cos_ref[...]
    sin = sin_ref[...]

    def rope(t):
        return t * cos + pltpu.roll(t, HEAD_DIM // 2, 1) * sin

    for h in range(ATTN_Q_HEADS):
        cols = slice(h * HEAD_DIM, (h + 1) * HEAD_DIM)
        q_ref[:, cols] = rope(p[:, cols]).astype(BF16)
    for h in range(ATTN_KV_HEADS):
        cols = slice(h * HEAD_DIM, (h + 1) * HEAD_DIM)
        src = slice(ATTN_Q_DIM + h * HEAD_DIM, ATTN_Q_DIM + (h + 1) * HEAD_DIM)
        k_ref[:, cols] = rope(p[:, src]).astype(BF16)
    v_ref[...] = p[:, ATTN_Q_DIM + ATTN_KV_DIM:].astype(BF16)


def _attn_in(xs, gains, w_in, cos, sin, seq):
    n = sum(x.shape[0] for x in xs)
    tm = min(TM_PROJ, seq)
    per_seq = seq // tm
    x_specs, starts = _row_tile_specs(xs, tm)
    return pl.pallas_call(
        functools.partial(_attn_in_kernel, starts=starts),
        out_shape=(jax.ShapeDtypeStruct((n, ATTN_Q_DIM), BF16),
                   jax.ShapeDtypeStruct((n, ATTN_KV_DIM), BF16),
                   jax.ShapeDtypeStruct((n, ATTN_KV_DIM), BF16)),
        grid=(n // tm,),
        in_specs=x_specs + [
            _resident((4, D_MODEL)),
            _resident(w_in.shape),
            pl.BlockSpec((tm, HEAD_DIM), lambda i: (i % per_seq, 0)),
            pl.BlockSpec((tm, HEAD_DIM), lambda i: (i % per_seq, 0)),
        ],
        out_specs=(pl.BlockSpec((tm, ATTN_Q_DIM), lambda i: (i, 0)),
                   pl.BlockSpec((tm, ATTN_KV_DIM), lambda i: (i, 0)),
                   pl.BlockSpec((tm, ATTN_KV_DIM), lambda i: (i, 0))),
        compiler_params=_params(1),
        name="attn_in",
    )(*xs, gains, w_in, cos, sin)


def _attn_kernel(sink_ref, q_ref, kp_ref, km_ref, kn_ref, vp_ref, vm_ref, vn_ref, o_ref, *, tq, seq):
    t = pl.program_id(0)
    p0 = (t * tq) % seq
    prev_bad = p0 == 0
    next_bad = (p0 + tq) == seq
    kfull = jnp.concatenate([kp_ref[...], km_ref[...], kn_ref[...]], axis=0)
    vfull = jnp.concatenate([vp_ref[...], vm_ref[...], vn_ref[...]], axis=0)

    rows = ATTN_GROUP * ATTN_BLOCK
    cols = 3 * ATTN_BLOCK
    r = lax.broadcasted_iota(jnp.int32, (rows, cols), 0) % ATTN_BLOCK
    cc = lax.broadcasted_iota(jnp.int32, (rows, cols), 1)
    cb = cc // ATTN_BLOCK
    c = cc % ATTN_BLOCK
    band_ok = ((cb == 0) & (c >= r)) | (cb == 1) | ((cb == 2) & (c <= r))
    nblk = tq // ATTN_BLOCK
    index = [(jb, h) for jb in range(nblk) for h in range(ATTN_KV_HEADS)]

    def head_cols(h, g):
        return slice((h * ATTN_GROUP + g) * HEAD_DIM, (h * ATTN_GROUP + g + 1) * HEAD_DIM)

    def key_rows(jb):
        return slice(jb * ATTN_BLOCK, jb * ATTN_BLOCK + cols)

    def valid(jb):
        ok = band_ok
        if jb == 0:
            ok = ok & (cc >= jnp.where(prev_bad, ATTN_BLOCK, 0))
        if jb == nblk - 1:
            ok = ok & (cc < jnp.where(next_bad, 2 * ATTN_BLOCK, cols))
        return ok

    def scores(part):
        q3 = jnp.stack([jnp.concatenate([q_ref[jb * ATTN_BLOCK:(jb + 1) * ATTN_BLOCK, head_cols(h, g)]
                                         for g in range(ATTN_GROUP)], axis=0) for jb, h in part])
        k3 = jnp.stack([kfull[key_rows(jb), h * HEAD_DIM:(h + 1) * HEAD_DIM] for jb, h in part])
        return _bdot_nt(q3, k3) * (HEAD_DIM ** -0.5)

    def finish(part, s):
        s = jnp.stack([jnp.where(valid(jb), s[i], -jnp.inf) for i, (jb, h) in enumerate(part)])
        sink = jnp.stack([jnp.concatenate(
            [jnp.full((ATTN_BLOCK, 1), sink_ref[h * ATTN_GROUP + g], F32) for g in range(ATTN_GROUP)],
            axis=0) for jb, h in part])
        m = jnp.maximum(jnp.max(s, axis=-1, keepdims=True), sink)
        p = jnp.exp(s - m)
        denom = jnp.sum(p, axis=-1, keepdims=True) + jnp.exp(sink - m)
        v3 = jnp.stack([vfull[key_rows(jb), h * HEAD_DIM:(h + 1) * HEAD_DIM] for jb, h in part])
        o = _bdot((p / denom).astype(BF16), v3)
        for i, (jb, h) in enumerate(part):
            for g in range(ATTN_GROUP):
                o_ref[jb * ATTN_BLOCK:(jb + 1) * ATTN_BLOCK, head_cols(h, g)] = (
                    o[i, g * ATTN_BLOCK:(g + 1) * ATTN_BLOCK, :].astype(BF16))

    parts = [index[i:i + ATTN_PART] for i in range(0, len(index), ATTN_PART)]
    all_scores = [scores(part) for part in parts]
    for part, s in zip(parts, all_scores):
        finish(part, s)


def _attn(q, k, v, sinks, seq):
    n = q.shape[0]
    tq = min(TQ_ATTN, seq)
    per = tq // ATTN_BLOCK
    last = n // ATTN_BLOCK - 1
    prev_map = lambda t: (jnp.maximum(t * per - 1, 0), 0)
    next_map = lambda t: (jnp.minimum((t + 1) * per, last), 0)
    kv_main = pl.BlockSpec((tq, ATTN_KV_DIM), lambda t: (t, 0))
    kv_prev = pl.BlockSpec((ATTN_BLOCK, ATTN_KV_DIM), prev_map)
    kv_next = pl.BlockSpec((ATTN_BLOCK, ATTN_KV_DIM), next_map)
    return pl.pallas_call(
        functools.partial(_attn_kernel, tq=tq, seq=seq),
        out_shape=jax.ShapeDtypeStruct((n, ATTN_Q_DIM), BF16),
        grid=(n // tq,),
        in_specs=[
            pl.BlockSpec(memory_space=pltpu.SMEM),
            pl.BlockSpec((tq, ATTN_Q_DIM), lambda t: (t, 0)),
            kv_prev, kv_main, kv_next, kv_prev, kv_main, kv_next,
        ],
        out_specs=pl.BlockSpec((tq, ATTN_Q_DIM), lambda t: (t, 0)),
        compiler_params=_params(1),
        name="attn_core",
    )(sinks, q, k, k, k, v, v, v)


def _gdn_in_kernel(xp_ref, xm_ref, xn_ref, g_ref, wc_ref, wz_ref, wg_ref, cw_ref, gp_ref,
                   q_ref, k_ref, kt_ref, v_ref, gz_ref, gc_ref, gr_ref, *, tm, seq):
    i = pl.program_id(0)
    p0 = (i * tm) % seq
    gain = g_ref[0:1, :]
    xp = jnp.where(p0 == 0, 0.0, xp_ref[...])
    xn = jnp.where(p0 + tm == seq, 0.0, xn_ref[...])
    hn = _rms(jnp.concatenate([xp, xm_ref[...], xn], axis=0), gain).astype(BF16)
    hm = hn[HALO:HALO + tm, :]

    qk_scale = HEAD_DIM ** -0.5
    for c in range(GDN_CONV_DIM // CONV_COLS):
        cols = slice(c * CONV_COLS, (c + 1) * CONV_COLS)
        pc = _dot(hn, wc_ref[:, cols])
        acc = None
        for j in range(CONV_WIDTH):
            lo = HALO - CONV_PAD + j
            term = pc[lo:lo + tm, :] * cw_ref[j:j + 1, cols]
            acc = term if acc is None else acc + term
        act = _silu(acc)
        for hh in range(CONV_COLS // HEAD_DIM):
            col0 = c * CONV_COLS + hh * HEAD_DIM
            t = act[:, hh * HEAD_DIM:(hh + 1) * HEAD_DIM]
            if col0 < 2 * GDN_QK_DIM:
                t = t * lax.rsqrt(jnp.sum(t * t, axis=-1, keepdims=True) + RMS_EPS)
            if col0 < GDN_QK_DIM:
                q_ref[:, col0:col0 + HEAD_DIM] = (t * qk_scale).astype(BF16)
            elif col0 < 2 * GDN_QK_DIM:
                kc0 = col0 - GDN_QK_DIM
                k_ref[:, kc0:kc0 + HEAD_DIM] = t.astype(BF16)
                kt_ref[0, kc0:kc0 + HEAD_DIM, :] = t.T
            else:
                vc0 = col0 - 2 * GDN_QK_DIM
                v_ref[:, vc0:vc0 + HEAD_DIM] = t.astype(BF16)

    for c in range(GDN_V_DIM // CONV_COLS):
        cols = slice(c * CONV_COLS, (c + 1) * CONV_COLS)
        gz_ref[:, cols] = _silu(_dot(hm, wz_ref[:, cols]))

    ga = _dot(hm, wg_ref[...])
    lane = lax.broadcasted_iota(jnp.int32, (1, GATE_LANES), 1)
    is_decay = (lane & 1) != 0
    is_bwd = (lane & 2) != 0
    beta = jax.nn.sigmoid(ga)
    pre = ga + gp_ref[1:2, :]
    softplus = jnp.maximum(pre, 0.0) + jnp.log1p(jnp.exp(-jnp.abs(pre)))
    gval = jnp.where(is_decay, -jnp.exp(gp_ref[0:1, :]) * softplus, 0.0)
    rr = lax.broadcasted_iota(jnp.int32, (tm, tm), 0)
    cc = lax.broadcasted_iota(jnp.int32, (tm, tm), 1)
    same = (rr // CHUNK) == (cc // CHUNK)
    tri_f = (same & (rr >= cc)).astype(F32)
    tri_b = (same & (rr <= cc)).astype(F32)
    cum_f = jnp.dot(tri_f, gval, preferred_element_type=F32, precision=lax.Precision.HIGHEST)
    cum_b = jnp.dot(tri_b, gval, preferred_element_type=F32, precision=lax.Precision.HIGHEST)
    gates = jnp.where(is_decay, jnp.where(is_bwd, cum_b, cum_f), beta)
    gc_ref[...] = gates
    gates_t = gates.T
    low_half = lax.broadcasted_iota(jnp.int32, (4, 2 * CHUNK), 1) < CHUNK
    for kh in range(GDN_K_HEADS):
        for m in range(tm // (2 * CHUNK)):
            tok = slice(m * 2 * CHUNK, (m + 1) * 2 * CHUNK)
            head_a = gates_t[8 * kh:8 * kh + 4, tok]
            head_b = gates_t[8 * kh + 4:8 * kh + 8, tok]
            gr_ref[0, kh, 2 * m] = jnp.where(low_half, head_a, pltpu.roll(head_b, CHUNK, 1))
            gr_ref[0, kh, 2 * m + 1] = jnp.where(low_half, pltpu.roll(head_a, CHUNK, 1), head_b)


def _gdn_in(x, gains, w_conv, w_z, w_gate, conv_w, gate_params, seq):
    n = x.shape[0]
    batch = n // seq
    tm = min(TM_PROJ, seq)
    per_seq = seq // tm
    per = tm // HALO
    last = n // HALO - 1
    row = lambda i: (i, 0)
    return pl.pallas_call(
        functools.partial(_gdn_in_kernel, tm=tm, seq=seq),
        out_shape=(jax.ShapeDtypeStruct((n, GDN_QK_DIM), BF16),
                   jax.ShapeDtypeStruct((n, GDN_QK_DIM), BF16),
                   jax.ShapeDtypeStruct((batch, GDN_QK_DIM, seq), F32),
                   jax.ShapeDtypeStruct((n, GDN_V_DIM), BF16),
                   jax.ShapeDtypeStruct((n, GDN_V_DIM), F32),
                   jax.ShapeDtypeStruct((n, GATE_LANES), F32),
                   jax.ShapeDtypeStruct((batch, GDN_K_HEADS, seq // CHUNK, 4, 2 * CHUNK), F32)),
        grid=(n // tm,),
        in_specs=[
            pl.BlockSpec((HALO, D_MODEL), lambda i: (jnp.maximum(i * per - 1, 0), 0)),
            pl.BlockSpec((tm, D_MODEL), row),
            pl.BlockSpec((HALO, D_MODEL), lambda i: (jnp.minimum((i + 1) * per, last), 0)),
            _resident((4, D_MODEL)),
            _resident(w_conv.shape),
            _resident(w_z.shape),
            _resident(w_gate.shape),
            _resident(conv_w.shape),
            _resident(gate_params.shape),
        ],
        out_specs=(pl.BlockSpec((tm, GDN_QK_DIM), row),
                   pl.BlockSpec((tm, GDN_QK_DIM), row),
                   pl.BlockSpec((1, GDN_QK_DIM, tm), lambda i: (i // per_seq, 0, i % per_seq)),
                   pl.BlockSpec((tm, GDN_V_DIM), row),
                   pl.BlockSpec((tm, GDN_V_DIM), row),
                   pl.BlockSpec((tm, GATE_LANES), row),
                   pl.BlockSpec((1, GDN_K_HEADS, tm // CHUNK, 4, 2 * CHUNK),
                                lambda i: (i // per_seq, 0, i % per_seq, 0, 0))),
        compiler_params=_params(1),
        name="gdn_in",
    )(x, x, x, gains, w_conv, w_z, w_gate, conv_w, gate_params)


def _pair_diag(x):
    half = x.shape[2] // 2
    low = lax.broadcasted_iota(jnp.int32, x.shape, 2) < half
    return jnp.concatenate([jnp.where(low, x, 0.0), jnp.where(low, 0.0, x)], axis=1)


def _lane_diag(x, zero):
    half = x.shape[2] // 2
    return jnp.concatenate([jnp.concatenate([x[:, :, :half], zero], axis=2),
                            jnp.concatenate([zero, x[:, :, half:]], axis=2)], axis=1)


def _pair_tri_inverse_minus_eye(l_mat, r, c):
    e = -jnp.where((r // 2) == (c // 2), l_mat, 0.0)
    b = 2
    while b < CHUNK:
        cb = jnp.where(((r // (2 * b)) == (c // (2 * b))) & ((r // b) != (c // b)), l_mat, 0.0)
        z = cb + _bdot(cb.astype(BF16), _pair_diag(e).astype(BF16))
        y = z + _bdot(e.astype(BF16), _pair_diag(z).astype(BF16))
        e = e - y
        b *= 2
    return e


def _gdn_scan_kernel(*refs, tb, nb, reverse, final):
    if final:
        (q_ref, k_ref, kt_ref, v_ref, gc_ref, gr_ref, of_ref, gz_ref, nw_ref, o_ref, s_ref) = refs
    else:
        (q_ref, k_ref, kt_ref, v_ref, gc_ref, gr_ref, o_ref, s_ref) = refs

    @pl.when(pl.program_id(0) == 0)
    def _():
        s_ref[...] = jnp.zeros(s_ref.shape, F32)

    nc = tb // CHUNK
    kh_n = GDN_K_HEADS
    index = [(b, ci, j) for b in range(nb) for ci in range(nc) for j in range(kh_n)]
    npair = len(index)
    d = 1 if reverse else 0
    last = 0 if reverse else CHUNK - 1

    def rows(ci):
        return slice(ci * CHUNK, (ci + 1) * CHUNK)

    def head(j, width=HEAD_DIM):
        return slice(j * width, (j + 1) * width)

    r = lax.broadcasted_iota(jnp.int32, (1, CHUNK, 2 * CHUNK), 1)
    lane = lax.broadcasted_iota(jnp.int32, (1, CHUNK, 2 * CHUNK), 2)
    c = lane % CHUNK
    low = lane < CHUNK
    incl = (r <= c) if reverse else (r >= c)
    strict = (r < c) if reverse else (r > c)
    eye = (r == c).astype(F32)

    q3 = jnp.stack([q_ref[b, rows(ci), head(j)] for b, ci, j in index])
    k3 = jnp.stack([k_ref[b, rows(ci), head(j)] for b, ci, j in index])
    v3 = jnp.stack([v_ref[b, rows(ci), head(j, 2 * HEAD_DIM)] for b, ci, j in index])
    prod = _bdot_nt(jnp.concatenate([q3, k3], axis=1), jnp.concatenate([k3, k3], axis=1))
    qk = prod[:, :CHUNK]
    kk = prod[:, CHUNK:]

    def g_column(b, ci, j):
        la = 8 * j + 2 * d + 1
        col_a = jnp.broadcast_to(gc_ref[b, rows(ci), la:la + 1], (CHUNK, 2 * CHUNK))
        col_b = jnp.broadcast_to(gc_ref[b, rows(ci), la + 4:la + 5], (CHUNK, 2 * CHUNK))
        return jnp.where(low[0], col_a, col_b)

    g_col = jnp.stack([g_column(b, ci, j) for b, ci, j in index])
    g_row = jnp.stack([gr_ref[b, j, ci, 2 * d + 1:2 * d + 2, :] for b, ci, j in index])
    b_row = jnp.stack([gr_ref[b, j, ci, 2 * d:2 * d + 1, :] for b, ci, j in index])

    decay = jnp.exp(jnp.where(incl, g_col - g_row, -jnp.inf))
    l_mat = jnp.where(strict, kk * decay, 0.0) * b_row
    t_mat = _pair_tri_inverse_minus_eye(l_mat, r, c) + eye
    eg_row = jnp.exp(g_row)
    zero_c = jnp.zeros((npair, CHUNK, HEAD_DIM), BF16)
    u_all = _bdot(t_mat.astype(BF16), _lane_diag(v3, zero_c))
    w_all = _bdot((t_mat * eg_row).astype(BF16),
                  _lane_diag(jnp.concatenate([k3, k3], axis=2), zero_c))
    aq = qk * decay * b_row
    dg = eye * pltpu.roll(eg_row.reshape(npair, 2 * CHUNK), CHUNK, 1)[:, None, :]
    lhs_o_a = jnp.where(low, aq, dg).astype(BF16)
    lhs_o_b = jnp.where(low, dg, aq).astype(BF16)

    g_last = jnp.where(low[:, :1], jnp.broadcast_to(g_row[:, :, last:last + 1], g_row.shape),
                       jnp.broadcast_to(g_row[:, :, CHUNK + last:CHUNK + last + 1], g_row.shape))
    kscale = jnp.exp(g_last - g_row) * b_row
    e_last = jnp.exp(g_last)
    e_last = jnp.concatenate([jnp.broadcast_to(e_last[:, :, :1], (npair, 1, HEAD_DIM)),
                              jnp.broadcast_to(e_last[:, :, CHUNK:CHUNK + 1], (npair, 1, HEAD_DIM))],
                             axis=2)

    ks2 = kscale.reshape(npair, 2 * CHUNK)
    ks2_rolled = pltpu.roll(ks2, CHUNK, 1)
    low2 = low[0, :1]
    zero2 = jnp.zeros_like(ks2)
    odd = jnp.stack([jnp.full((1, 2 * CHUNK), ci % 2 == 1) for b, ci, j in index]).reshape(npair, 2 * CHUNK)
    srow_a = jnp.where(odd, jnp.where(low2, zero2, ks2_rolled), jnp.where(low2, ks2, zero2))
    srow_b = jnp.where(odd, jnp.where(low2, zero2, ks2), jnp.where(low2, ks2_rolled, zero2))
    kt_tile = jnp.stack([kt_ref[b, head(j), (ci // 2) * 2 * CHUNK:(ci // 2 + 1) * 2 * CHUNK]
                         for b, ci, j in index])
    kgt_a = (kt_tile * srow_a[:, None, :]).astype(BF16)
    kgt_b = (kt_tile * srow_b[:, None, :]).astype(BF16)

    def of_chunk(x, ci):
        x = x.reshape((nb, nc, kh_n) + x.shape[1:])
        return x[:, ci].reshape((nb * kh_n,) + x.shape[3:])

    state = s_ref[...]
    groups = nb * kh_n
    zero_g = jnp.zeros((groups, CHUNK, HEAD_DIM), BF16)
    order = range(nc - 1, -1, -1) if reverse else range(nc)
    for ci in order:
        q_c = of_chunk(q3, ci)
        w_c = of_chunk(w_all, ci).astype(BF16)
        u_c = of_chunk(u_all, ci)
        s16 = state.astype(BF16)
        r_a = _bdot(jnp.concatenate([w_c[:, :, :HEAD_DIM], q_c], axis=1), s16[:, :, :HEAD_DIM])
        r_b = _bdot(jnp.concatenate([w_c[:, :, HEAD_DIM:], q_c], axis=1), s16[:, :, HEAD_DIM:])
        vn_a = (u_c[:, :, :HEAD_DIM] - r_a[:, :CHUNK]).astype(BF16)
        vn_b = (u_c[:, :, HEAD_DIM:] - r_b[:, :CHUNK]).astype(BF16)
        o_a = _bdot(of_chunk(lhs_o_a, ci), jnp.concatenate([vn_a, r_a[:, CHUNK:].astype(BF16)], axis=1))
        o_b = _bdot(of_chunk(lhs_o_b, ci), jnp.concatenate([r_b[:, CHUNK:].astype(BF16), vn_b], axis=1))
        o = jnp.concatenate([o_a, o_b], axis=2)
        pad_a = [vn_a, zero_g] if ci % 2 == 0 else [zero_g, vn_a]
        pad_b = [vn_b, zero_g] if ci % 2 == 0 else [zero_g, vn_b]
        upd = jnp.concatenate([_bdot(of_chunk(kgt_a, ci), jnp.concatenate(pad_a, axis=1)),
                               _bdot(of_chunk(kgt_b, ci), jnp.concatenate(pad_b, axis=1))], axis=2)
        state = state * of_chunk(e_last, ci) + upd
        for b in range(nb):
            for j in range(kh_n):
                o_bj = o[b * kh_n + j]
                cols = head(j, 2 * HEAD_DIM)
                if final:
                    tot = o_bj + of_ref[b, rows(ci), cols]
                    y = jnp.concatenate([_rms(tot[:, :HEAD_DIM], nw_ref[...]),
                                         _rms(tot[:, HEAD_DIM:], nw_ref[...])], axis=1)
                    o_ref[b, rows(ci), cols] = (y * gz_ref[b, rows(ci), cols]).astype(BF16)
                else:
                    o_ref[b, rows(ci), cols] = o_bj
    s_ref[...] = state


def _gdn_scan(q, k, kt, v, gc, gr, *, reverse, o_fwd=None, gz=None, norm_w=None):
    batch, seq, _ = q.shape
    final = o_fwd is not None
    tb = min(TB_SCAN, seq)
    nblk = seq // tb
    nc = tb // CHUNK
    blk = (lambda i: nblk - 1 - i) if reverse else (lambda i: i)
    tok = lambda width: pl.BlockSpec((batch, tb, width), lambda i: (0, blk(i), 0))
    in_specs = [tok(GDN_QK_DIM), tok(GDN_QK_DIM),
                pl.BlockSpec((batch, GDN_QK_DIM, tb), lambda i: (0, 0, blk(i))),
                tok(GDN_V_DIM), tok(GATE_LANES),
                pl.BlockSpec((batch, GDN_K_HEADS, nc, 4, 2 * CHUNK), lambda i: (0, 0, blk(i), 0, 0))]
    args = [q, k, kt, v, gc, gr]
    if final:
        in_specs += [tok(GDN_V_DIM), tok(GDN_V_DIM), _resident((1, HEAD_DIM))]
        args += [o_fwd, gz, norm_w]
    return pl.pallas_call(
        functools.partial(_gdn_scan_kernel, tb=tb, nb=batch, reverse=reverse, final=final),
        out_shape=jax.ShapeDtypeStruct((batch, seq, GDN_V_DIM), BF16 if final else F32),
        grid=(nblk,),
        in_specs=in_specs,
        out_specs=tok(GDN_V_DIM),
        scratch_shapes=[pltpu.VMEM((batch * GDN_K_HEADS, HEAD_DIM, 2 * HEAD_DIM), F32)],
        compiler_params=_params(1),
        name="gdn_scan_bwd" if reverse else "gdn_scan_fwd",
    )(*args)


def _rope_tables(seq):
    half = HEAD_DIM // 2
    inv_freq = 1.0 / (ROPE_THETA ** (jnp.arange(half, dtype=F32) / half))
    ang = jnp.arange(seq, dtype=F32)[:, None] * inv_freq[None, :]
    cos = jnp.cos(ang)
    sin = jnp.sin(ang)
    return jnp.concatenate([cos, cos], axis=-1), jnp.concatenate([-sin, sin], axis=-1)


def _gate_lane_order():
    kh = jnp.arange(GDN_K_HEADS)[:, None, None, None]
    a = jnp.arange(2)[None, :, None, None]
    d = jnp.arange(2)[None, None, :, None]
    kind = jnp.arange(2)[None, None, None, :]
    return (d * 2 * GDN_V_HEADS + kind * GDN_V_HEADS + 2 * kh + a).reshape(-1)


def _gate_vector(p):
    t = p.astype(F32).reshape(2, GDN_K_HEADS, 2).transpose(1, 2, 0)
    t = jnp.stack([jnp.zeros_like(t), t], axis=3)
    return jnp.pad(t.reshape(-1), (0, GATE_LANES - 4 * GDN_V_HEADS))


def _trunk(xs, seq, norm_gains, attn_w_in, attn_sinks, attn_w_out, gdn_w_in, gdn_conv_w, gdn_A_log,
           gdn_dt_bias, gdn_norm_w, gdn_w_out, mlp_w_up, mlp_w_down):
    depth = norm_gains.shape[0]
    out_rows = [x.shape[0] for x in xs]
    n = sum(out_rows)
    batch = n // seq
    cos, sin = _rope_tables(seq)
    lane_order = _gate_lane_order()
    for i in range(depth):
        j = i // 2
        gains = norm_gains[i].astype(F32)
        if i % 2 == 0:
            q, k, v = _attn_in(xs, gains, attn_w_in[j].astype(BF16), cos, sin, seq)
            mixed = _attn(q, k, v, attn_sinks[j].astype(F32), seq)
            w_out = attn_w_out[j]
        else:
            (x,) = xs
            w = gdn_w_in[j]
            w_gate = w[:, GDN_CONV_DIM + GDN_V_DIM:][:, lane_order]
            w_gate = jnp.pad(w_gate, ((0, 0), (0, GATE_LANES - w_gate.shape[1])))
            gate_params = jnp.stack([_gate_vector(gdn_A_log[j]), _gate_vector(gdn_dt_bias[j])])
            q, k, kt, v, gz, gc, gr = _gdn_in(
                x, gains, w[:, :GDN_CONV_DIM].astype(BF16),
                w[:, GDN_CONV_DIM:GDN_CONV_DIM + GDN_V_DIM].astype(BF16), w_gate.astype(BF16),
                gdn_conv_w[j].astype(F32), gate_params, seq)
            by_seq = lambda t: t.reshape(batch, seq, t.shape[-1])
            q, k, v, gz, gc = by_seq(q), by_seq(k), by_seq(v), by_seq(gz), by_seq(gc)
            o_fwd = _gdn_scan(q, k, kt, v, gc, gr, reverse=False)
            mixed = _gdn_scan(q, k, kt, v, gc, gr, reverse=True, o_fwd=o_fwd, gz=gz,
                              norm_w=gdn_norm_w[j].astype(F32).reshape(1, HEAD_DIM))
            mixed = mixed.reshape(n, GDN_V_DIM)
            w_out = gdn_w_out[j]
        weights = (w_out.astype(BF16), gains, mlp_w_up[i].astype(BF16), mlp_w_down[i].astype(BF16))
        if i < depth - 1:
            xs = [_out_mlp(mixed, xs, *weights)]
        else:
            outs, row0 = [], 0
            for rows in out_rows:
                outs.append(_out_mlp(mixed, xs, *weights, row0=row0, nrows=rows))
                row0 += rows
            return outs


def kernel(x_prompt, x_sample, norm_gains, attn_w_in, attn_sinks, attn_w_out, gdn_w_in, gdn_conv_w,
           gdn_A_log, gdn_dt_bias, gdn_norm_w, gdn_w_out, mlp_w_up, mlp_w_down):
    seq = x_prompt.shape[1]
    assert x_sample.shape[1] == seq and x_prompt.shape[2] == D_MODEL
    xs = [x_prompt.reshape(-1, D_MODEL), x_sample.reshape(-1, D_MODEL)]
    y_prompt, y_sample = _trunk(xs, seq, norm_gains, attn_w_in, attn_sinks, attn_w_out, gdn_w_in,
                                gdn_conv_w, gdn_A_log, gdn_dt_bias, gdn_norm_w, gdn_w_out, mlp_w_up,
                                mlp_w_down)
    return (y_prompt.reshape(x_prompt.shape), y_sample.reshape(x_sample.shape))
```

```python
import functools

import jax
import jax.numpy as jnp
from jax import lax
from jax.experimental import pallas as pl
from jax.experimental.pallas import tpu as pltpu

F32 = jnp.float32
BF16 = jnp.bfloat16

D_MODEL = 1024
D_FF = 4 * D_MODEL
RMS_EPS = 1e-6
HEAD_DIM = 128

ATTN_Q_HEADS = 8
ATTN_KV_HEADS = 2
ATTN_GROUP = ATTN_Q_HEADS // ATTN_KV_HEADS
ATTN_Q_DIM = ATTN_Q_HEADS * HEAD_DIM
ATTN_KV_DIM = ATTN_KV_HEADS * HEAD_DIM
ATTN_BLOCK = 128
ROPE_THETA = 10000.0

GDN_K_HEADS = 8
GDN_V_HEADS = 16
GDN_QK_DIM = GDN_K_HEADS * HEAD_DIM
GDN_V_DIM = GDN_V_HEADS * HEAD_DIM
GDN_CONV_DIM = 2 * GDN_QK_DIM + GDN_V_DIM
CONV_WIDTH = 5
CONV_PAD = CONV_WIDTH // 2
CHUNK = 64
GATE_LANES = 128
HALO = 16

VMEM_LIMIT_BYTES = 56 * 1024 * 1024

TM_PROJ = 512
TQ_ATTN = 2048
TB_SCAN = 128
ATTN_PART = 2
FF_CHUNK = 512
CONV_COLS = 512


def _rms(x, w):
    return x * lax.rsqrt(jnp.mean(x * x, axis=-1, keepdims=True) + RMS_EPS) * w


def _silu(x):
    return x * jax.nn.sigmoid(x)


def _dot(a, b):
    return jnp.dot(a, b, preferred_element_type=F32)


def _bdot(a, b):
    return lax.dot_general(a, b, (((2,), (1,)), ((0,), (0,))), preferred_element_type=F32)


def _bdot_nt(a, b):
    return lax.dot_general(a, b, (((2,), (2,)), ((0,), (0,))), preferred_element_type=F32)


def _resident(shape):
    zeros = (0,) * len(shape)
    return pl.BlockSpec(shape, lambda *_: zeros, pipeline_mode=pl.Buffered(1))


def _row_tile_specs(arrays, tm, first_tile=0):
    specs, starts, start = [], [], 0
    for arr in arrays:
        ntiles = arr.shape[0] // tm
        specs.append(pl.BlockSpec(
            (tm, arr.shape[1]),
            lambda i, s=start, n=ntiles: (jnp.clip(i + first_tile - s, 0, n - 1), 0)))
        starts.append(start - first_tile)
        start += ntiles
    return specs, tuple(starts)


def _pick_row_tile(refs, starts):
    i = pl.program_id(0)
    x = refs[0][...]
    for ref, start in zip(refs[1:], starts[1:]):
        x = jnp.where(i >= start, ref[...], x)
    return x


def _params(n_axes):
    return pltpu.CompilerParams(dimension_semantics=("arbitrary",) * n_axes,
                                vmem_limit_bytes=VMEM_LIMIT_BYTES)


def _out_mlp_kernel(*refs, starts):
    x_refs = refs[1:1 + len(starts)]
    a_ref = refs[0]
    wo_ref, g_ref, wu_ref, wd_ref, o_ref = refs[1 + len(starts):]
    h = _dot(a_ref[...], wo_ref[...])
    x1 = _pick_row_tile(x_refs, starts) + _rms(h, g_ref[1:2, :])
    hn = _rms(x1, g_ref[2:3, :]).astype(BF16)
    acc = jnp.zeros(x1.shape, F32)
    for c in range(D_FF // FF_CHUNK):
        cols = slice(c * FF_CHUNK, (c + 1) * FF_CHUNK)
        up = _dot(hn, wu_ref[:, cols])
        act = jnp.square(jnp.maximum(up, 0.0)).astype(BF16)
        acc = acc + _dot(act, wd_ref[cols, :])
    o_ref[...] = x1 + _rms(acc, g_ref[3:4, :])


def _out_mlp(a, xs, w_out, gains, w_up, w_down, row0=0, nrows=None):
    n, k = a.shape
    nrows = n - row0 if nrows is None else nrows
    tm = min(TM_PROJ, nrows)
    first = row0 // tm
    x_specs, starts = _row_tile_specs(xs, tm, first)
    return pl.pallas_call(
        functools.partial(_out_mlp_kernel, starts=starts),
        out_shape=jax.ShapeDtypeStruct((nrows, D_MODEL), F32),
        grid=(nrows // tm,),
        in_specs=[pl.BlockSpec((tm, k), lambda i: (i + first, 0))] + x_specs + [
            _resident((k, D_MODEL)),
            _resident((4, D_MODEL)),
            _resident((D_MODEL, D_FF)),
            _resident((D_FF, D_MODEL)),
        ],
        out_specs=pl.BlockSpec((tm, D_MODEL), lambda i: (i, 0)),
        compiler_params=_params(1),
        name="out_mlp",
    )(a, *xs, w_out, gains, w_up, w_down)


def _attn_in_kernel(*refs, starts):
    x_refs = refs[:len(starts)]
    g_ref, w_ref, cos_ref, sin_ref, q_ref, k_ref, v_ref = refs[len(starts):]
    hn = _rms(_pick_row_tile(x_refs, starts), g_ref[0:1, :]).astype(BF16)
    p = _dot(hn, w_ref[...])
    cos = cos_ref[...]
    sin = sin_ref[...]

    def rope(t):
        return t * cos + pltpu.roll(t, HEAD_DIM // 2, 1) * sin

    for h in range(ATTN_Q_HEADS):
        cols = slice(h * HEAD_DIM, (h + 1) * HEAD_DIM)
        q_ref[:, cols] = rope(p[:, cols]).astype(BF16)
    for h in range(ATTN_KV_HEADS):
        cols = slice(h * HEAD_DIM, (h + 1) * HEAD_DIM)
        src = slice(ATTN_Q_DIM + h * HEAD_DIM, ATTN_Q_DIM + (h + 1) * HEAD_DIM)
        k_ref[:, cols] = rope(p[:, src]).astype(BF16)
    v_ref[...] = p[:, ATTN_Q_DIM + ATTN_KV_DIM:].astype(BF16)


def _attn_in(xs, gains, w_in, cos, sin, seq):
    n = sum(x.shape[0] for x in xs)
    tm = min(TM_PROJ, seq)
    per_seq = seq // tm
    x_specs, starts = _row_tile_specs(xs, tm)
    return pl.pallas_call(
        functools.partial(_attn_in_kernel, starts=starts),
        out_shape=(jax.ShapeDtypeStruct((n, ATTN_Q_DIM), BF16),
                   jax.ShapeDtypeStruct((n, ATTN_KV_DIM), BF16),
                   jax.ShapeDtypeStruct((n, ATTN_KV_DIM), BF16)),
        grid=(n // tm,),
        in_specs=x_specs + [
            _resident((4, D_MODEL)),
            _resident(w_in.shape),
            pl.BlockSpec((tm, HEAD_DIM), lambda i: (i % per_seq, 0)),
            pl.BlockSpec((tm, HEAD_DIM), lambda i: (i % per_seq, 0)),
        ],
        out_specs=(pl.BlockSpec((tm, ATTN_Q_DIM), lambda i: (i, 0)),
                   pl.BlockSpec((tm, ATTN_KV_DIM), lambda i: (i, 0)),
                   pl.BlockSpec((tm, ATTN_KV_DIM), lambda i: (i, 0))),
        compiler_params=_params(1),
        name="attn_in",
    )(*xs, gains, w_in, cos, sin)


def _attn_kernel(sink_ref, q_ref, kp_ref, km_ref, kn_ref, vp_ref, vm_ref, vn_ref, o_ref, *, tq, seq):
    t = pl.program_id(0)
    p0 = (t * tq) % seq
    prev_bad = p0 == 0
    next_bad = (p0 + tq) == seq
    kfull = jnp.concatenate([kp_ref[...], km_ref[...], kn_ref[...]], axis=0)
    vfull = jnp.concatenate([vp_ref[...], vm_ref[...], vn_ref[...]], axis=0)

    rows = ATTN_GROUP * ATTN_BLOCK
    cols = 3 * ATTN_BLOCK
    r = lax.broadcasted_iota(jnp.int32, (rows, cols), 0) % ATTN_BLOCK
    cc = lax.broadcasted_iota(jnp.int32, (rows, cols), 1)
    cb = cc // ATTN_BLOCK
    c = cc % ATTN_BLOCK
    band_ok = ((cb == 0) & (c >= r)) | (cb == 1) | ((cb == 2) & (c <= r))
    nblk = tq // ATTN_BLOCK
    index = [(jb, h) for jb in range(nblk) for h in range(ATTN_KV_HEADS)]

    def head_cols(h, g):
        return slice((h * ATTN_GROUP + g) * HEAD_DIM, (h * ATTN_GROUP + g + 1) * HEAD_DIM)

    def key_rows(jb):
        return slice(jb * ATTN_BLOCK, jb * ATTN_BLOCK + cols)

    def valid(jb):
        ok = band_ok
        if jb == 0:
            ok = ok & (cc >= jnp.where(prev_bad, ATTN_BLOCK, 0))
        if jb == nblk - 1:
            ok = ok & (cc < jnp.where(next_bad, 2 * ATTN_BLOCK, cols))
        return ok

    def scores(part):
        q3 = jnp.stack([jnp.concatenate([q_ref[jb * ATTN_BLOCK:(jb + 1) * ATTN_BLOCK, head_cols(h, g)]
                                         for g in range(ATTN_GROUP)], axis=0) for jb, h in part])
        k3 = jnp.stack([kfull[key_rows(jb), h * HEAD_DIM:(h + 1) * HEAD_DIM] for jb, h in part])
        return _bdot_nt(q3, k3) * (HEAD_DIM ** -0.5)

    def finish(part, s):
        s = jnp.stack([jnp.where(valid(jb), s[i], -jnp.inf) for i, (jb, h) in enumerate(part)])
        sink = jnp.stack([jnp.concatenate(
            [jnp.full((ATTN_BLOCK, 1), sink_ref[h * ATTN_GROUP + g], F32) for g in range(ATTN_GROUP)],
            axis=0) for jb, h in part])
        m = jnp.maximum(jnp.max(s, axis=-1, keepdims=True), sink)
        p = jnp.exp(s - m)
        denom = jnp.sum(p, axis=-1, keepdims=True) + jnp.exp(sink - m)
        v3 = jnp.stack([vfull[key_rows(jb), h * HEAD_DIM:(h + 1) * HEAD_DIM] for jb, h in part])
        o = _bdot((p / denom).astype(BF16), v3)
        for i, (jb, h) in enumerate(part):
            for g in range(ATTN_GROUP):
                o_ref[jb * ATTN_BLOCK:(jb + 1) * ATTN_BLOCK, head_cols(h, g)] = (
                    o[i, g * ATTN_BLOCK:(g + 1) * ATTN_BLOCK, :].astype(BF16))

    parts = [index[i:i + ATTN_PART] for i in range(0, len(index), ATTN_PART)]
    all_scores = [scores(part) for part in parts]
    for part, s in zip(parts, all_scores):
        finish(part, s)


def _attn(q, k, v, sinks, seq):
    n = q.shape[0]
    tq = min(TQ_ATTN, seq)
    per = tq // ATTN_BLOCK
    last = n // ATTN_BLOCK - 1
    prev_map = lambda t: (jnp.maximum(t * per - 1, 0), 0)
    next_map = lambda t: (jnp.minimum((t + 1) * per, last), 0)
    kv_main = pl.BlockSpec((tq, ATTN_KV_DIM), lambda t: (t, 0))
    kv_prev = pl.BlockSpec((ATTN_BLOCK, ATTN_KV_DIM), prev_map)
    kv_next = pl.BlockSpec((ATTN_BLOCK, ATTN_KV_DIM), next_map)
    return pl.pallas_call(
        functools.partial(_attn_kernel, tq=tq, seq=seq),
        out_shape=jax.ShapeDtypeStruct((n, ATTN_Q_DIM), BF16),
        grid=(n // tq,),
        in_specs=[
            pl.BlockSpec(memory_space=pltpu.SMEM),
            pl.BlockSpec((tq, ATTN_Q_DIM), lambda t: (t, 0)),
            kv_prev, kv_main, kv_next, kv_prev, kv_main, kv_next,
        ],
        out_specs=pl.BlockSpec((tq, ATTN_Q_DIM), lambda t: (t, 0)),
        compiler_params=_params(1),
        name="attn_core",
    )(sinks, q, k, k, k, v, v, v)


def _gdn_in_kernel(xp_ref, xm_ref, xn_ref, g_ref, wc_ref, wz_ref, wg_ref, cw_ref, gp_ref,
                   q_ref, k_ref, kt_ref, v_ref, gz_ref, gc_ref, gr_ref, *, tm, seq):
    i = pl.program_id(0)
    p0 = (i * tm) % seq
    gain = g_ref[0:1, :]
    xp = jnp.where(p0 == 0, 0.0, xp_ref[...])
    xn = jnp.where(p0 + tm == seq, 0.0, xn_ref[...])
    hn = _rms(jnp.concatenate([xp, xm_ref[...], xn], axis=0), gain).astype(BF16)
    hm = hn[HALO:HALO + tm, :]

    qk_scale = HEAD_DIM ** -0.5
    for c in range(GDN_CONV_DIM // CONV_COLS):
        cols = slice(c * CONV_COLS, (c + 1) * CONV_COLS)
        pc = _dot(hn, wc_ref[:, cols])
        acc = None
        for j in range(CONV_WIDTH):
            lo = HALO - CONV_PAD + j
            term = pc[lo:lo + tm, :] * cw_ref[j:j + 1, cols]
            acc = term if acc is None else acc + term
        act = _silu(acc)
        for hh in range(CONV_COLS // HEAD_DIM):
            col0 = c * CONV_COLS + hh * HEAD_DIM
            t = act[:, hh * HEAD_DIM:(hh + 1) * HEAD_DIM]
            if col0 < 2 * GDN_QK_DIM:
                t = t * lax.rsqrt(jnp.sum(t * t, axis=-1, keepdims=True) + RMS_EPS)
            if col0 < GDN_QK_DIM:
                q_ref[:, col0:col0 + HEAD_DIM] = (t * qk_scale).astype(BF16)
            elif col0 < 2 * GDN_QK_DIM:
                kc0 = col0 - GDN_QK_DIM
                k_ref[:, kc0:kc0 + HEAD_DIM] = t.astype(BF16)
                kt_ref[0, kc0:kc0 + HEAD_DIM, :] = t.T
            else:
                vc0 = col0 - 2 * GDN_QK_DIM
                v_ref[:, vc0:vc0 + HEAD_DIM] = t.astype(BF16)

    for c in range(GDN_V_DIM // CONV_COLS):
        cols = slice(c * CONV_COLS, (c + 1) * CONV_COLS)
        gz_ref[:, cols] = _silu(_dot(hm, wz_ref[:, cols]))

    ga = _dot(hm, wg_ref[...])
    lane = lax.broadcasted_iota(jnp.int32, (1, GATE_LANES), 1)
    is_decay = (lane & 1) != 0
    is_bwd = (lane & 2) != 0
    beta = jax.nn.sigmoid(ga)
    pre = ga + gp_ref[1:2, :]
    softplus = jnp.maximum(pre, 0.0) + jnp.log1p(jnp.exp(-jnp.abs(pre)))
    gval = jnp.where(is_decay, -jnp.exp(gp_ref[0:1, :]) * softplus, 0.0)
    rr = lax.broadcasted_iota(jnp.int32, (tm, tm), 0)
    cc = lax.broadcasted_iota(jnp.int32, (tm, tm), 1)
    same = (rr // CHUNK) == (cc // CHUNK)
    tri_f = (same & (rr >= cc)).astype(F32)
    tri_b = (same & (rr <= cc)).astype(F32)
    cum_f = jnp.dot(tri_f, gval, preferred_element_type=F32, precision=lax.Precision.HIGHEST)
    cum_b = jnp.dot(tri_b, gval, preferred_element_type=F32, precision=lax.Precision.HIGHEST)
    gates = jnp.where(is_decay, jnp.where(is_bwd, cum_b, cum_f), beta)
    gc_ref[...] = gates
    gates_t = gates.T
    low_half = lax.broadcasted_iota(jnp.int32, (4, 2 * CHUNK), 1) < CHUNK
    for kh in range(GDN_K_HEADS):
        for m in range(tm // (2 * CHUNK)):
            tok = slice(m * 2 * CHUNK, (m + 1) * 2 * CHUNK)
            head_a = gates_t[8 * kh:8 * kh + 4, tok]
            head_b = gates_t[8 * kh + 4:8 * kh + 8, tok]
            gr_ref[0, kh, 2 * m] = jnp.where(low_half, head_a, pltpu.roll(head_b, CHUNK, 1))
            gr_ref[0, kh, 2 * m + 1] = jnp.where(low_half, pltpu.roll(head_a, CHUNK, 1), head_b)


def _gdn_in(x, gains, w_conv, w_z, w_gate, conv_w, gate_params, seq):
    n = x.shape[0]
    batch = n // seq
    tm = min(TM_PROJ, seq)
    per_seq = seq // tm
    per = tm // HALO
    last = n // HALO - 1
    row = lambda i: (i, 0)
    return pl.pallas_call(
        functools.partial(_gdn_in_kernel, tm=tm, seq=seq),
        out_shape=(jax.ShapeDtypeStruct((n, GDN_QK_DIM), BF16),
                   jax.ShapeDtypeStruct((n, GDN_QK_DIM), BF16),
                   jax.ShapeDtypeStruct((batch, GDN_QK_DIM, seq), F32),
                   jax.ShapeDtypeStruct((n, GDN_V_DIM), BF16),
                   jax.ShapeDtypeStruct((n, GDN_V_DIM), F32),
                   jax.ShapeDtypeStruct((n, GATE_LANES), F32),
                   jax.ShapeDtypeStruct((batch, GDN_K_HEADS, seq // CHUNK, 4, 2 * CHUNK), F32)),
        grid=(n // tm,),
        in_specs=[
            pl.BlockSpec((HALO, D_MODEL), lambda i: (jnp.maximum(i * per - 1, 0), 0)),
            pl.BlockSpec((tm, D_MODEL), row),
            pl.BlockSpec((HALO, D_MODEL), lambda i: (jnp.minimum((i + 1) * per, last), 0)),
            _resident((4, D_MODEL)),
            _resident(w_conv.shape),
            _resident(w_z.shape),
            _resident(w_gate.shape),
            _resident(conv_w.shape),
            _resident(gate_params.shape),
        ],
        out_specs=(pl.BlockSpec((tm, GDN_QK_DIM), row),
                   pl.BlockSpec((tm, GDN_QK_DIM), row),
                   pl.BlockSpec((1, GDN_QK_DIM, tm), lambda i: (i // per_seq, 0, i % per_seq)),
                   pl.BlockSpec((tm, GDN_V_DIM), row),
                   pl.BlockSpec((tm, GDN_V_DIM), row),
                   pl.BlockSpec((tm, GATE_LANES), row),
                   pl.BlockSpec((1, GDN_K_HEADS, tm // CHUNK, 4, 2 * CHUNK),
                                lambda i: (i // per_seq, 0, i % per_seq, 0, 0))),
        compiler_params=_params(1),
        name="gdn_in",
    )(x, x, x, gains, w_conv, w_z, w_gate, conv_w, gate_params)


def _pair_diag(x):
    half = x.shape[2] // 2
    low = lax.broadcasted_iota(jnp.int32, x.shape, 2) < half
    return jnp.concatenate([jnp.where(low, x, 0.0), jnp.where(low, 0.0, x)], axis=1)


def _lane_diag(x, zero):
    half = x.shape[2] // 2
    return jnp.concatenate([jnp.concatenate([x[:, :, :half], zero], axis=2),
                            jnp.concatenate([zero, x[:, :, half:]], axis=2)], axis=1)


def _pair_tri_inverse_minus_eye(l_mat, r, c):
    e = -jnp.where((r // 2) == (c // 2), l_mat, 0.0)
    b = 2
    while b < CHUNK:
        cb = jnp.where(((r // (2 * b)) == (c // (2 * b))) & ((r // b) != (c // b)), l_mat, 0.0)
        z = cb + _bdot(cb.astype(BF16), _pair_diag(e).astype(BF16))
        y = z + _bdot(e.astype(BF16), _pair_diag(z).astype(BF16))
        e = e - y
        b *= 2
    return e


def _gdn_scan_kernel(*refs, tb, nb, reverse, final):
    if final:
        (q_ref, k_ref, kt_ref, v_ref, gc_ref, gr_ref, of_ref, gz_ref, nw_ref, o_ref, s_ref) = refs
    else:
        (q_ref, k_ref, kt_ref, v_ref, gc_ref, gr_ref, o_ref, s_ref) = refs

    @pl.when(pl.program_id(0) == 0)
    def _():
        s_ref[...] = jnp.zeros(s_ref.shape, F32)

    nc = tb // CHUNK
    kh_n = GDN_K_HEADS
    index = [(b, ci, j) for b in range(nb) for ci in range(nc) for j in range(kh_n)]
    d = 1 if reverse else 0
    last = 0 if reverse else CHUNK - 1

    def rows(ci):
        return slice(ci * CHUNK, (ci + 1) * CHUNK)

    def head(j, width=HEAD_DIM):
        return slice(j * width, (j + 1) * width)

    r = lax.broadcasted_iota(jnp.int32, (1, CHUNK, 2 * CHUNK), 1)
    lane = lax.broadcasted_iota(jnp.int32, (1, CHUNK, 2 * CHUNK), 2)
    c = lane % CHUNK
    low = lane < CHUNK
    incl = (r <= c) if reverse else (r >= c)
    strict = (r < c) if reverse else (r > c)
    eye = (r == c).astype(F32)

    def prepare(index):
        npair = len(index)
        q3 = jnp.stack([q_ref[b, rows(ci), head(j)] for b, ci, j in index])
        k3 = jnp.stack([k_ref[b, rows(ci), head(j)] for b, ci, j in index])
        v3 = jnp.stack([v_ref[b, rows(ci), head(j, 2 * HEAD_DIM)] for b, ci, j in index])
        prod = _bdot_nt(jnp.concatenate([q3, k3], axis=1), jnp.concatenate([k3, k3], axis=1))
        qk = prod[:, :CHUNK]
        kk = prod[:, CHUNK:]

        def g_column(b, ci, j):
            la = 8 * j + 2 * d + 1
            col_a = jnp.broadcast_to(gc_ref[b, rows(ci), la:la + 1], (CHUNK, 2 * CHUNK))
            col_b = jnp.broadcast_to(gc_ref[b, rows(ci), la + 4:la + 5], (CHUNK, 2 * CHUNK))
            return jnp.where(low[0], col_a, col_b)

        g_col = jnp.stack([g_column(b, ci, j) for b, ci, j in index])
        g_row = jnp.stack([gr_ref[b, j, ci, 2 * d + 1:2 * d + 2, :] for b, ci, j in index])
        b_row = jnp.stack([gr_ref[b, j, ci, 2 * d:2 * d + 1, :] for b, ci, j in index])

        decay = jnp.exp(jnp.where(incl, g_col - g_row, -jnp.inf))
        l_mat = jnp.where(strict, kk * decay, 0.0) * b_row
        t_mat = _pair_tri_inverse_minus_eye(l_mat, r, c) + eye
        eg_row = jnp.exp(g_row)
        zero_c = jnp.zeros((npair, CHUNK, HEAD_DIM), BF16)
        u_all = _bdot(t_mat.astype(BF16), _lane_diag(v3, zero_c))
        w_all = _bdot((t_mat * eg_row).astype(BF16),
                      _lane_diag(jnp.concatenate([k3, k3], axis=2), zero_c))
        aq = qk * decay * b_row
        dg = eye * pltpu.roll(eg_row.reshape(npair, 2 * CHUNK), CHUNK, 1)[:, None, :]
        lhs_o_a = jnp.where(low, aq, dg).astype(BF16)
        lhs_o_b = jnp.where(low, dg, aq).astype(BF16)

        g_last = jnp.where(low[:, :1], jnp.broadcast_to(g_row[:, :, last:last + 1], g_row.shape),
                           jnp.broadcast_to(g_row[:, :, CHUNK + last:CHUNK + last + 1], g_row.shape))
        kscale = jnp.exp(g_last - g_row) * b_row
        e_last = jnp.exp(g_last)
        e_last = jnp.concatenate([jnp.broadcast_to(e_last[:, :, :1], (npair, 1, HEAD_DIM)),
                                  jnp.broadcast_to(e_last[:, :, CHUNK:CHUNK + 1], (npair, 1, HEAD_DIM))],
                                 axis=2)

        ks2 = kscale.reshape(npair, 2 * CHUNK)
        ks2_rolled = pltpu.roll(ks2, CHUNK, 1)
        low2 = low[0, :1]
        zero2 = jnp.zeros_like(ks2)
        odd = jnp.stack([jnp.full((1, 2 * CHUNK), ci % 2 == 1) for b, ci, j in index]).reshape(npair, 2 * CHUNK)
        srow_a = jnp.where(odd, jnp.where(low2, zero2, ks2_rolled), jnp.where(low2, ks2, zero2))
        srow_b = jnp.where(odd, jnp.where(low2, zero2, ks2), jnp.where(low2, ks2_rolled, zero2))
        kt_tile = jnp.stack([kt_ref[b, head(j), (ci // 2) * 2 * CHUNK:(ci // 2 + 1) * 2 * CHUNK]
                             for b, ci, j in index])
        kgt_a = (kt_tile * srow_a[:, None, :]).astype(BF16)
        kgt_b = (kt_tile * srow_b[:, None, :]).astype(BF16)
        return q3, u_all, w_all, lhs_o_a, lhs_o_b, e_last, kgt_a, kgt_b

    per_seq = len(index) // nb
    pieces = [prepare(index[i:i + per_seq]) for i in range(0, len(index), per_seq)]
    q3, u_all, w_all, lhs_o_a, lhs_o_b, e_last, kgt_a, kgt_b = (
        jnp.concatenate(parts, axis=0) for parts in zip(*pieces))

    def of_chunk(x, ci):
        x = x.reshape((nb, nc, kh_n) + x.shape[1:])
        return x[:, ci].reshape((nb * kh_n,) + x.shape[3:])

    state = s_ref[...]
    groups = nb * kh_n
    zero_g = jnp.zeros((groups, CHUNK, HEAD_DIM), BF16)
    order = range(nc - 1, -1, -1) if reverse else range(nc)
    for ci in order:
        q_c = of_chunk(q3, ci)
        w_c = of_chunk(w_all, ci).astype(BF16)
        u_c = of_chunk(u_all, ci)
        s16 = state.astype(BF16)
        r_a = _bdot(jnp.concatenate([w_c[:, :, :HEAD_DIM], q_c], axis=1), s16[:, :, :HEAD_DIM])
        r_b = _bdot(jnp.concatenate([w_c[:, :, HEAD_DIM:], q_c], axis=1), s16[:, :, HEAD_DIM:])
        vn_a = (u_c[:, :, :HEAD_DIM] - r_a[:, :CHUNK]).astype(BF16)
        vn_b = (u_c[:, :, HEAD_DIM:] - r_b[:, :CHUNK]).astype(BF16)
        o_a = _bdot(of_chunk(lhs_o_a, ci), jnp.concatenate([vn_a, r_a[:, CHUNK:].astype(BF16)], axis=1))
        o_b = _bdot(of_chunk(lhs_o_b, ci), jnp.concatenate([r_b[:, CHUNK:].astype(BF16), vn_b], axis=1))
        o = jnp.concatenate([o_a, o_b], axis=2)
        pad_a = [vn_a, zero_g] if ci % 2 == 0 else [zero_g, vn_a]
        pad_b = [vn_b, zero_g] if ci % 2 == 0 else [zero_g, vn_b]
        upd = jnp.concatenate([_bdot(of_chunk(kgt_a, ci), jnp.concatenate(pad_a, axis=1)),
                               _bdot(of_chunk(kgt_b, ci), jnp.concatenate(pad_b, axis=1))], axis=2)
        state = state * of_chunk(e_last, ci) + upd
        for b in range(nb):
            for j in range(kh_n):
                o_bj = o[b * kh_n + j]
                cols = head(j, 2 * HEAD_DIM)
                if final:
                    tot = o_bj + of_ref[b, rows(ci), cols]
                    y = jnp.concatenate([_rms(tot[:, :HEAD_DIM], nw_ref[...]),
                                         _rms(tot[:, HEAD_DIM:], nw_ref[...])], axis=1)
                    o_ref[b, rows(ci), cols] = (y * gz_ref[b, rows(ci), cols]).astype(BF16)
                else:
                    o_ref[b, rows(ci), cols] = o_bj
    s_ref[...] = state


def _gdn_scan(q, k, kt, v, gc, gr, *, reverse, o_fwd=None, gz=None, norm_w=None):
    batch, seq, _ = q.shape
    final = o_fwd is not None
    tb = min(TB_SCAN, seq)
    nblk = seq // tb
    nc = tb // CHUNK
    blk = (lambda i: nblk - 1 - i) if reverse else (lambda i: i)
    tok = lambda width: pl.BlockSpec((batch, tb, width), lambda i: (0, blk(i), 0))
    in_specs = [tok(GDN_QK_DIM), tok(GDN_QK_DIM),
                pl.BlockSpec((batch, GDN_QK_DIM, tb), lambda i: (0, 0, blk(i))),
                tok(GDN_V_DIM), tok(GATE_LANES),
                pl.BlockSpec((batch, GDN_K_HEADS, nc, 4, 2 * CHUNK), lambda i: (0, 0, blk(i), 0, 0))]
    args = [q, k, kt, v, gc, gr]
    if final:
        in_specs += [tok(GDN_V_DIM), tok(GDN_V_DIM), _resident((1, HEAD_DIM))]
        args += [o_fwd, gz, norm_w]
    return pl.pallas_call(
        functools.partial(_gdn_scan_kernel, tb=tb, nb=batch, reverse=reverse, final=final),
        out_shape=jax.ShapeDtypeStruct((batch, seq, GDN_V_DIM), BF16 if final else F32),
        grid=(nblk,),
        in_specs=in_specs,
        out_specs=tok(GDN_V_DIM),
        scratch_shapes=[pltpu.VMEM((batch * GDN_K_HEADS, HEAD_DIM, 2 * HEAD_DIM), F32)],
        compiler_params=_params(1),
        name="gdn_scan_bwd" if reverse else "gdn_scan_fwd",
    )(*args)


def _rope_tables(seq):
    half = HEAD_DIM // 2
    inv_freq = 1.0 / (ROPE_THETA ** (jnp.arange(half, dtype=F32) / half))
    ang = jnp.arange(seq, dtype=F32)[:, None] * inv_freq[None, :]
    cos = jnp.cos(ang)
    sin = jnp.sin(ang)
    return jnp.concatenate([cos, cos], axis=-1), jnp.concatenate([-sin, sin], axis=-1)


def _gate_lane_order():
    kh = jnp.arange(GDN_K_HEADS)[:, None, None, None]
    a = jnp.arange(2)[None, :, None, None]
    d = jnp.arange(2)[None, None, :, None]
    kind = jnp.arange(2)[None, None, None, :]
    return (d * 2 * GDN_V_HEADS + kind * GDN_V_HEADS + 2 * kh + a).reshape(-1)


def _gate_vector(p):
    t = p.astype(F32).reshape(2, GDN_K_HEADS, 2).transpose(1, 2, 0)
    t = jnp.stack([jnp.zeros_like(t), t], axis=3)
    return jnp.pad(t.reshape(-1), (0, GATE_LANES - 4 * GDN_V_HEADS))


def _trunk(xs, seq, norm_gains, attn_w_in, attn_sinks, attn_w_out, gdn_w_in, gdn_conv_w, gdn_A_log,
           gdn_dt_bias, gdn_norm_w, gdn_w_out, mlp_w_up, mlp_w_down):
    depth = norm_gains.shape[0]
    out_rows = [x.shape[0] for x in xs]
    n = sum(out_rows)
    batch = n // seq
    cos, sin = _rope_tables(seq)
    lane_order = _gate_lane_order()
    for i in range(depth):
        j = i // 2
        gains = norm_gains[i].astype(F32)
        if i % 2 == 0:
            q, k, v = _attn_in(xs, gains, attn_w_in[j].astype(BF16), cos, sin, seq)
            mixed = _attn(q, k, v, attn_sinks[j].astype(F32), seq)
            w_out = attn_w_out[j]
        else:
            (x,) = xs
            w = gdn_w_in[j]
            w_gate = w[:, GDN_CONV_DIM + GDN_V_DIM:][:, lane_order]
            w_gate = jnp.pad(w_gate, ((0, 0), (0, GATE_LANES - w_gate.shape[1])))
            gate_params = jnp.stack([_gate_vector(gdn_A_log[j]), _gate_vector(gdn_dt_bias[j])])
            q, k, kt, v, gz, gc, gr = _gdn_in(
                x, gains, w[:, :GDN_CONV_DIM].astype(BF16),
                w[:, GDN_CONV_DIM:GDN_CONV_DIM + GDN_V_DIM].astype(BF16), w_gate.astype(BF16),
                gdn_conv_w[j].astype(F32), gate_params, seq)
            by_seq = lambda t: t.reshape(batch, seq, t.shape[-1])
            q, k, v, gz, gc = by_seq(q), by_seq(k), by_seq(v), by_seq(gz), by_seq(gc)
            o_fwd = _gdn_scan(q, k, kt, v, gc, gr, reverse=False)
            mixed = _gdn_scan(q, k, kt, v, gc, gr, reverse=True, o_fwd=o_fwd, gz=gz,
                              norm_w=gdn_norm_w[j].astype(F32).reshape(1, HEAD_DIM))
            mixed = mixed.reshape(n, GDN_V_DIM)
            w_out = gdn_w_out[j]
        weights = (w_out.astype(BF16), gains, mlp_w_up[i].astype(BF16), mlp_w_down[i].astype(BF16))
        if i < depth - 1:
            xs = [_out_mlp(mixed, xs, *weights)]
        else:
            outs, row0 = [], 0
            for rows in out_rows:
                outs.append(_out_mlp(mixed, xs, *weights, row0=row0, nrows=rows))
                row0 += rows
            return outs


def kernel(x_prompt, x_sample, norm_gains, attn_w_in, attn_sinks, attn_w_out, gdn_w_in, gdn_conv_w,
           gdn_A_log, gdn_dt_bias, gdn_norm_w, gdn_w_out, mlp_w_up, mlp_w_down):
    seq = x_prompt.shape[1]
    assert x_sample.shape[1] == seq and x_prompt.shape[2] == D_MODEL
    xs = [x_prompt.reshape(-1, D_MODEL), x_sample.reshape(-1, D_MODEL)]
    y_prompt, y_sample = _trunk(xs, seq, norm_gains, attn_w_in, attn_sinks, attn_w_out, gdn_w_in,
                                gdn_conv_w, gdn_A_log, gdn_dt_bias, gdn_norm_w, gdn_w_out, mlp_w_up,
                                mlp_w_down)
    return (y_prompt.reshape(x_prompt.shape), y_sample.reshape(x_sample.shape))
```
